```python
import math
import jax, jax.numpy as jnp
from jax import lax
import numpy as np


D_MODEL = 1024
BATCH = 8
SEQ = 4096
DEPTH = 1
DEC_BATCH = 4
DEC_SEQ = 8192
PAST_LEN = 128

MLA_HEADS = 8
QK_NOPE = 128
QK_ROPE = 64
QK_HEAD = QK_NOPE + QK_ROPE
V_HEAD = 128
Q_LORA = 384
KV_LORA = 256
MLA_WIDTH = MLA_HEADS * V_HEAD
ROPE_THETA = 10000.0
Q_BLOCK = 128

SSM_EXPAND = 2
SSM_INNER = SSM_EXPAND * D_MODEL
SSM_HEADDIM = 64
SSM_HEADS = SSM_INNER // SSM_HEADDIM
SSM_GROUPS = 4
SSM_STATE = 128
SSM_CONV_DIM = SSM_INNER + 2 * SSM_GROUPS * SSM_STATE
CONV_WIDTH = 5
CHUNK = 128

EPS = 1e-6

IN_SIZES = (Q_LORA, KV_LORA, QK_ROPE, MLA_WIDTH,
            SSM_INNER, SSM_CONV_DIM, SSM_HEADS, SSM_HEADS,
            D_MODEL, D_MODEL)
N_IN = Q_LORA + KV_LORA + QK_ROPE + MLA_WIDTH + SSM_INNER + SSM_CONV_DIM + 2 * SSM_HEADS + 2 * D_MODEL

kernel_name = 'hybrid_mla_ssd_gated_encoder'


def _split_cols(t, sizes):
    idx = []
    acc = 0
    for sz in sizes[:-1]:
        acc += sz
        idx.append(acc)
    return jnp.split(t, idx, axis=-1)


def _rms_norm(t, g):
    tf = t.astype(jnp.float32)
    tf = tf * lax.rsqrt(jnp.mean(tf * tf, axis=-1, keepdims=True) + EPS)
    return (tf * g.astype(jnp.float32)).astype(t.dtype)


def _rope_tables(s):
    half = QK_ROPE // 2
    inv_freq = jnp.exp(-math.log(ROPE_THETA) * jnp.arange(half, dtype=jnp.float32) / half)
    ang = jnp.arange(s, dtype=jnp.float32)[:, None] * inv_freq[None, :]
    return jnp.cos(ang), jnp.sin(ang)


def _apply_rope(t, cos, sin):
    half = QK_ROPE // 2
    tf = t.astype(jnp.float32)
    t1, t2 = tf[..., :half], tf[..., half:]
    c = cos[None, :, None, :]
    s_ = sin[None, :, None, :]
    return jnp.concatenate([t1 * c - t2 * s_, t2 * c + t1 * s_], axis=-1).astype(t.dtype)


def _mla_attention(q, k, v):
    b, s = q.shape[0], q.shape[1]
    nb = s // Q_BLOCK
    qb = q.reshape(b, nb, Q_BLOCK, MLA_HEADS, QK_HEAD).transpose(1, 0, 2, 3, 4)
    scale = QK_HEAD ** -0.5

    def block(qi):
        sc = jnp.einsum('bqhd,bkhd->bhqk', qi, k, preferred_element_type=jnp.float32) * scale
        p = jax.nn.softmax(sc, axis=-1)
        return jnp.einsum('bhqk,bkhd->bqhd', p.astype(v.dtype), v)

    o = lax.map(block, qb)
    return o.transpose(1, 0, 2, 3, 4).reshape(b, s, MLA_WIDTH)


def _dwconv_centred(t, w, bias):
    pad = CONV_WIDTH // 2
    ker = w.astype(t.dtype)[:, None, :]
    y = lax.conv_general_dilated(t, ker, window_strides=(1,), padding=[(pad, pad)],
                                 dimension_numbers=('NWC', 'WIO', 'NWC'),
                                 feature_group_count=t.shape[-1])
    return y + bias.astype(t.dtype)


def _ssd_chunked(x, dt, A, B, C, D):
    b, s, h, p = x.shape
    g, n = B.shape[2], B.shape[3]
    e = h // g
    nc = s // CHUNK
    xf = x.astype(jnp.float32)
    a = (dt * A[None, None, :]).reshape(b, nc, CHUNK, h)
    a_cum = jnp.cumsum(a, axis=2)
    xdt = (xf * dt[..., None]).reshape(b, nc, CHUNK, g, e, p)
    Bc = B.astype(jnp.float32).reshape(b, nc, CHUNK, g, n)
    Cc = C.astype(jnp.float32).reshape(b, nc, CHUNK, g, n)
    acl = a_cum.transpose(0, 1, 3, 2)
    diff = acl[..., :, None] - acl[..., None, :]
    tri = jnp.tril(jnp.ones((CHUNK, CHUNK), dtype=bool))
    Lm = jnp.exp(jnp.where(tri, diff, -jnp.inf)).reshape(b, nc, g, e, CHUNK, CHUNK)
    CB = jnp.einsum('bclgn,bcsgn->bcgls', Cc, Bc)
    y_diag = jnp.einsum('bcgls,bcgels,bcsgep->bclgep', CB, Lm, xdt)
    a_last = a_cum[:, :, -1:, :]
    decay_states = jnp.exp(a_last - a_cum).reshape(b, nc, CHUNK, g, e)
    states = jnp.einsum('bclgn,bclge,bclgep->bcgepn', Bc, decay_states, xdt)
    chunk_decay = jnp.exp(a_last[:, :, 0, :]).reshape(b, nc, g, e)

    def step(hstate, inp):
        dec, st = inp
        return dec[..., None, None] * hstate + st, hstate

    h0 = jnp.zeros((b, g, e, p, n), jnp.float32)
    _, prev = lax.scan(step, h0, (chunk_decay.transpose(1, 0, 2, 3),
                                  states.transpose(1, 0, 2, 3, 4, 5)))
    prev = prev.transpose(1, 0, 2, 3, 4, 5)
    y_off = jnp.einsum('bclgn,bcgepn,bclge->bclgep', Cc, prev,
                       jnp.exp(a_cum).reshape(b, nc, CHUNK, g, e))
    y = (y_diag + y_off).reshape(b, s, h, p) + xf * D.astype(jnp.float32)[None, None, :, None]
    return y


def _bidirectional_ssd(x, dt_f, dt_b, A_f, A_b, B, C, D_f, D_b):
    y_f = _ssd_chunked(x, dt_f, A_f, B, C, D_f)
    fl = lambda t: jnp.flip(t, axis=1)
    y_b = fl(_ssd_chunked(fl(x), fl(dt_b), A_b, fl(B), fl(C), D_b))
    return y_f + y_b


def _encoder_layer(x, c, norm_g, w_ada, b_ada, w_in, q_a_norm, w_q_up, kv_a_norm, w_kv_up,
                   q_norm, k_norm, w_proj_a, conv_w, conv_b, dt_bias_f, dt_bias_b,
                   a_log_f, a_log_b, d_f, d_b, ssm_norm, w_proj_b, w_out):
    b, s, _ = x.shape
    mod = jax.nn.silu(c) @ w_ada + b_ada
    shift, scale, gate = jnp.split(mod, 3, axis=-1)
    h = _rms_norm(x, norm_g) * (1.0 + scale[:, None, :]) + shift[:, None, :]

    proj = h @ w_in
    (q_c, kv_c, k_rope, gate_a, z, xbc, dt_f_raw, dt_b_raw, g_a, g_b) = _split_cols(proj, IN_SIZES)

    q = (_rms_norm(q_c, q_a_norm) @ w_q_up).reshape(b, s, MLA_HEADS, QK_HEAD)
    kv = (_rms_norm(kv_c, kv_a_norm) @ w_kv_up).reshape(b, s, MLA_HEADS, QK_NOPE + V_HEAD)
    k_nope, v = kv[..., :QK_NOPE], kv[..., QK_NOPE:]
    k = jnp.concatenate([k_nope, jnp.broadcast_to(k_rope[:, :, None, :], (b, s, MLA_HEADS, QK_ROPE))], axis=-1)
    q = _rms_norm(q, q_norm)
    k = _rms_norm(k, k_norm)
    cos, sin = _rope_tables(s)
    q = jnp.concatenate([q[..., :QK_NOPE], _apply_rope(q[..., QK_NOPE:], cos, sin)], axis=-1)
    k = jnp.concatenate([k[..., :QK_NOPE], _apply_rope(k[..., QK_NOPE:], cos, sin)], axis=-1)
    attn = _mla_attention(q, k, v)
    branch_a = (attn * jax.nn.silu(gate_a)) @ w_proj_a

    xbc = jax.nn.silu(_dwconv_centred(xbc, conv_w, conv_b))
    xs, Bm, Cm = _split_cols(xbc, (SSM_INNER, SSM_GROUPS * SSM_STATE, SSM_GROUPS * SSM_STATE))
    xs = xs.reshape(b, s, SSM_HEADS, SSM_HEADDIM)
    Bm = Bm.reshape(b, s, SSM_GROUPS, SSM_STATE)
    Cm = Cm.reshape(b, s, SSM_GROUPS, SSM_STATE)
    dt_f = jax.nn.softplus(dt_f_raw.astype(jnp.float32) + dt_bias_f.astype(jnp.float32))
    dt_b = jax.nn.softplus(dt_b_raw.astype(jnp.float32) + dt_bias_b.astype(jnp.float32))
    A_f = -jnp.exp(a_log_f.astype(jnp.float32))
    A_b = -jnp.exp(a_log_b.astype(jnp.float32))
    y = _bidirectional_ssd(xs, dt_f, dt_b, A_f, A_b, Bm, Cm, d_f, d_b)
    y = y.reshape(b, s, SSM_INNER).astype(x.dtype) * jax.nn.silu(z)
    y = _rms_norm(y.reshape(b, s, SSM_GROUPS, SSM_INNER // SSM_GROUPS),
                  ssm_norm.reshape(SSM_GROUPS, SSM_INNER // SSM_GROUPS)).reshape(b, s, SSM_INNER)
    branch_b = y @ w_proj_b

    merged = jax.nn.sigmoid(g_a) * branch_a + jax.nn.sigmoid(g_b) * branch_b
    out = merged @ w_out
    return (x + gate[:, None, :] * out).astype(x.dtype)


def setup_inputs(seed: int = 0) -> dict:
    key = jax.random.key(seed)
    ks = iter(jax.random.split(key, 48))

    def normal(shape, scale):
        return scale * jax.random.normal(next(ks), shape, jnp.float32)

    def gain(n):
        return 1.0 + 0.1 * jax.random.normal(next(ks), (DEPTH, n), jnp.float32)

    def dt_bias():
        u = jax.random.uniform(next(ks), (DEPTH, SSM_HEADS), jnp.float32)
        dt = jnp.exp(u * (math.log(0.1) - math.log(0.001)) + math.log(0.001))
        return dt + jnp.log(-jnp.expm1(-dt))

    def a_log():
        return jnp.log(jax.random.uniform(next(ks), (DEPTH, SSM_HEADS), jnp.float32, minval=1.0, maxval=16.0))

    x_prompt = normal((BATCH, SEQ, D_MODEL), 1.0)
    x_sample = normal((DEC_BATCH, DEC_SEQ, D_MODEL), 1.0)
    c_prompt = normal((BATCH, D_MODEL), 1.0)
    c_sample = normal((DEC_BATCH, D_MODEL), 1.0)
    return {
        'x_prompt': x_prompt,
        'x_sample': x_sample,
        'c_prompt': c_prompt,
        'c_sample': c_sample,
        'norm_g': gain(D_MODEL),
        'w_ada': normal((DEPTH, D_MODEL, 3 * D_MODEL), 0.5 * D_MODEL ** -0.5),
        'b_ada': normal((DEPTH, 3 * D_MODEL), 0.02),
        'w_in': normal((DEPTH, D_MODEL, N_IN), D_MODEL ** -0.5),
        'q_a_norm': gain(Q_LORA),
        'w_q_up': normal((DEPTH, Q_LORA, MLA_HEADS * QK_HEAD), Q_LORA ** -0.5),
        'kv_a_norm': gain(KV_LORA),
        'w_kv_up': normal((DEPTH, KV_LORA, MLA_HEADS * (QK_NOPE + V_HEAD)), KV_LORA ** -0.5),
        'q_norm': gain(QK_HEAD),
        'k_norm': gain(QK_HEAD),
        'w_proj_a': normal((DEPTH, MLA_WIDTH, D_MODEL), MLA_WIDTH ** -0.5),
        'conv_w': normal((DEPTH, CONV_WIDTH, SSM_CONV_DIM), CONV_WIDTH ** -0.5),
        'conv_b': normal((DEPTH, SSM_CONV_DIM), 0.02),
        'dt_bias_f': dt_bias(),
        'dt_bias_b': dt_bias(),
        'a_log_f': a_log(),
        'a_log_b': a_log(),
        'd_f': gain(SSM_HEADS),
        'd_b': gain(SSM_HEADS),
        'ssm_norm': gain(SSM_INNER),
        'w_proj_b': normal((DEPTH, SSM_INNER, D_MODEL), SSM_INNER ** -0.5),
        'w_out': normal((DEPTH, D_MODEL, D_MODEL), D_MODEL ** -0.5),
    }


def reference(x_prompt, x_sample, c_prompt, c_sample, norm_g, w_ada, b_ada, w_in, q_a_norm, w_q_up,
              kv_a_norm, w_kv_up, q_norm, k_norm, w_proj_a, conv_w, conv_b, dt_bias_f, dt_bias_b,
              a_log_f, a_log_b, d_f, d_b, ssm_norm, w_proj_b, w_out):
    layers = [(norm_g[l], w_ada[l], b_ada[l], w_in[l], q_a_norm[l], w_q_up[l], kv_a_norm[l], w_kv_up[l],
               q_norm[l], k_norm[l], w_proj_a[l], conv_w[l], conv_b[l], dt_bias_f[l], dt_bias_b[l],
               a_log_f[l], a_log_b[l], d_f[l], d_b[l], ssm_norm[l], w_proj_b[l], w_out[l])
              for l in range(DEPTH)]
    y_prompt = x_prompt
    y_sample = x_sample
    for l in range(DEPTH):
        y_prompt = _encoder_layer(y_prompt, c_prompt, *layers[l])
        y_sample = _encoder_layer(y_sample, c_sample, *layers[l])
    return (y_prompt, y_sample)
```

```python
import functools
import math

import jax
import jax.numpy as jnp
from jax import lax
from jax.experimental import pallas as pl
from jax.experimental.pallas import tpu as pltpu

D_MODEL = 1024
MLA_HEADS = 8
QK_NOPE = 128
QK_ROPE = 64
QK_HEAD = QK_NOPE + QK_ROPE
V_HEAD = 128
Q_LORA = 384
KV_LORA = 256
MLA_WIDTH = MLA_HEADS * V_HEAD
ROPE_THETA = 10000.0
HEAD_PAD = 256

SSM_INNER = 2048
SSM_HEADDIM = 64
SSM_HEADS = 32
SSM_GROUPS = 4
SSM_STATE = 128
SSM_GROUP_WIDTH = SSM_INNER // SSM_GROUPS
SSM_CONV_DIM = SSM_INNER + 2 * SSM_GROUPS * SSM_STATE
CONV_WIDTH = 5
CHUNK = 128
EPS = 1e-6

LANES = 128
CONV_HALO = 16
VMEM_LIMIT = 56 * 1024 * 1024

F32 = jnp.float32
BF16 = jnp.bfloat16
NEG_BIG = -1e30


def _dot(a, b):
    return jnp.dot(a, b, preferred_element_type=F32)


def _dot_nt(a, b):
    return lax.dot_general(a, b, (((1,), (1,)), ((), ())), preferred_element_type=F32)


def _sigmoid(t):
    return 1.0 / (1.0 + jnp.exp(-t))


def _silu(t):
    return t * _sigmoid(t)


def _softplus(t):
    return jnp.maximum(t, 0.0) + jnp.log1p(jnp.exp(-jnp.abs(t)))


def _rms(t):
    return t * lax.rsqrt(jnp.mean(t * t, axis=-1, keepdims=True) + EPS)


def _modulated_norm(x, g, scale, shift):
    return (_rms(x) * g) * (1.0 + scale) + shift


def _split_dot(m, v):
    hi = v.astype(BF16)
    r1 = v - hi.astype(F32)
    mid = r1.astype(BF16)
    lo = (r1 - mid.astype(F32)).astype(BF16)
    return _dot(m, hi) + _dot(m, mid) + _dot(m, lo)


def _params(*sem):
    return pltpu.CompilerParams(dimension_semantics=sem, vmem_limit_bytes=VMEM_LIMIT)


def _mod_kernel(c_ref, w_ref, b_ref, o_ref):
    o_ref[...] = _dot(_silu(c_ref[...]).astype(BF16), w_ref[...]) + b_ref[...]


def _mod(c, w_ada, b_ada):
    n = c.shape[0]
    return pl.pallas_call(
        _mod_kernel,
        grid=(3,),
        in_specs=[pl.BlockSpec((n, D_MODEL), lambda j: (0, 0)),
                  pl.BlockSpec((D_MODEL, D_MODEL), lambda j: (0, j)),
                  pl.BlockSpec((1, D_MODEL), lambda j: (0, j))],
        out_specs=pl.BlockSpec((n, D_MODEL), lambda j: (0, j)),
        out_shape=jax.ShapeDtypeStruct((n, 3 * D_MODEL), F32),
        compiler_params=_params("arbitrary"),
        name="mod",
    )(c, w_ada, b_ada)


def _attn_prep_kernel(x_ref, ng_ref, sc_ref, sh_ref, w1_ref, qan_ref, wq_ref, kvan_ref, wkv_ref,
                      gq_ref, gk_ref, cos_ref, sin_ref, q_ref, k_ref, v_ref):
    hb = _modulated_norm(x_ref[0], ng_ref[...], sc_ref[0], sh_ref[0]).astype(BF16)
    p1 = _dot(hb, w1_ref[...])
    qn = (_rms(p1[:, :Q_LORA]) * qan_ref[...]).astype(BF16)
    qa = _dot(qn, wq_ref[...])
    kvn = (_rms(p1[:, Q_LORA:Q_LORA + KV_LORA]) * kvan_ref[...]).astype(BF16)
    kv = _dot(kvn, wkv_ref[...])
    kr = p1[:, Q_LORA + KV_LORA:]
    cos = cos_ref[...]
    sin = sin_ref[...]
    lane = lax.broadcasted_iota(jnp.int32, cos.shape, 1)
    first_half = lane < QK_ROPE // 2

    def rope(t):
        swapped = jnp.where(first_half, pltpu.roll(t, LANES - QK_ROPE // 2, 1),
                            pltpu.roll(t, QK_ROPE // 2, 1))
        return t * cos + swapped * sin

    gq_nope, gq_tail = gq_ref[0:1, :], gq_ref[1:2, :]
    gk_nope, gk_tail = gk_ref[0:1, :], gk_ref[1:2, :]
    kr_rot = rope(kr * gk_tail)
    kr_ss = jnp.sum(kr * kr, axis=-1, keepdims=True)
    scale = QK_HEAD ** -0.5
    for h in range(MLA_HEADS):
        c0 = h * HEAD_PAD
        q_nope = qa[:, c0:c0 + QK_NOPE]
        q_tail = qa[:, c0 + QK_NOPE:c0 + HEAD_PAD]
        ss = jnp.sum(q_nope * q_nope, axis=-1, keepdims=True) + jnp.sum(q_tail * q_tail, axis=-1, keepdims=True)
        r = lax.rsqrt(ss * (1.0 / QK_HEAD) + EPS) * scale
        q_ref[0, :, c0:c0 + QK_NOPE] = (q_nope * gq_nope * r).astype(BF16)
        q_ref[0, :, c0 + QK_NOPE:c0 + HEAD_PAD] = (rope(q_tail * gq_tail) * r).astype(BF16)
        k_nope = kv[:, h * QK_NOPE:(h + 1) * QK_NOPE]
        ssk = jnp.sum(k_nope * k_nope, axis=-1, keepdims=True) + kr_ss
        rk = lax.rsqrt(ssk * (1.0 / QK_HEAD) + EPS)
        k_ref[0, :, c0:c0 + QK_NOPE] = (k_nope * gk_nope * rk).astype(BF16)
        k_ref[0, :, c0 + QK_NOPE:c0 + HEAD_PAD] = (kr_rot * rk).astype(BF16)
        v0 = MLA_HEADS * QK_NOPE + h * V_HEAD
        v_ref[0, :, h * V_HEAD:(h + 1) * V_HEAD] = kv[:, v0:v0 + V_HEAD].astype(BF16)


def _attn_prep(x, ng, scale, shift, w1, qan, wq, kvan, wkv, gq, gk, cos, sin, tm):
    b, s, _ = x.shape
    full = lambda a: pl.BlockSpec(a.shape, lambda i, j: (0,) * a.ndim)
    per_batch = pl.BlockSpec((1, 1, D_MODEL), lambda i, j: (i, 0, 0))
    return pl.pallas_call(
        _attn_prep_kernel,
        grid=(b, s // tm),
        in_specs=[pl.BlockSpec((1, tm, D_MODEL), lambda i, j: (i, j, 0)),
                  full(ng), per_batch, per_batch, full(w1), full(qan), full(wq), full(kvan), full(wkv),
                  full(gq), full(gk),
                  pl.BlockSpec((tm, LANES), lambda i, j: (j, 0)),
                  pl.BlockSpec((tm, LANES), lambda i, j: (j, 0))],
        out_specs=[pl.BlockSpec((1, tm, MLA_HEADS * HEAD_PAD), lambda i, j: (i, j, 0)),
                   pl.BlockSpec((1, tm, MLA_HEADS * HEAD_PAD), lambda i, j: (i, j, 0)),
                   pl.BlockSpec((1, tm, MLA_WIDTH), lambda i, j: (i, j, 0))],
        out_shape=[jax.ShapeDtypeStruct((b, s, MLA_HEADS * HEAD_PAD), BF16),
                   jax.ShapeDtypeStruct((b, s, MLA_HEADS * HEAD_PAD), BF16),
                   jax.ShapeDtypeStruct((b, s, MLA_WIDTH), BF16)],
        compiler_params=_params("arbitrary", "arbitrary"),
        name="attn_prep",
    )(x, ng, scale, shift, w1, qan, wq, kvan, wkv, gq, gk, cos, sin)


def _attention_kernel(q_ref, k_ref, v_ref, o_ref, m_ref, l_ref, acc_ref, *, tk):
    q = q_ref[0]
    m_ref[...] = jnp.full(m_ref.shape, NEG_BIG, F32)
    l_ref[...] = jnp.zeros(l_ref.shape, F32)
    acc_ref[...] = jnp.zeros(acc_ref.shape, F32)

    def body(j, carry):
        r0 = pl.multiple_of(j * tk, tk)
        s = _dot_nt(q, k_ref[0, pl.ds(r0, tk), :])
        m_old = m_ref[...]
        m_new = jnp.maximum(m_old, jnp.max(s, axis=-1, keepdims=True))
        alpha = jnp.exp(m_old - m_new)
        p = jnp.exp(s - m_new)
        l_ref[...] = alpha * l_ref[...] + jnp.sum(p, axis=-1, keepdims=True)
        acc_ref[...] = alpha * acc_ref[...] + _dot(p.astype(BF16), v_ref[0, pl.ds(r0, tk), :])
        m_ref[...] = m_new
        return carry

    lax.fori_loop(0, k_ref.shape[1] // tk, body, 0)
    o_ref[0] = (acc_ref[...] / l_ref[...]).astype(BF16)


def _attention(q, k, v, tq, tk):
    b, s, _ = q.shape
    return pl.pallas_call(
        functools.partial(_attention_kernel, tk=tk),
        grid=(b, MLA_HEADS, s // tq),
        in_specs=[pl.BlockSpec((1, tq, HEAD_PAD), lambda i, h, j: (i, j, h)),
                  pl.BlockSpec((1, s, HEAD_PAD), lambda i, h, j: (i, 0, h)),
                  pl.BlockSpec((1, s, V_HEAD), lambda i, h, j: (i, 0, h))],
        out_specs=pl.BlockSpec((1, tq, V_HEAD), lambda i, h, j: (i, j, h)),
        out_shape=jax.ShapeDtypeStruct((b, s, MLA_WIDTH), BF16),
        scratch_shapes=[pltpu.VMEM((tq, 1), F32), pltpu.VMEM((tq, 1), F32), pltpu.VMEM((tq, V_HEAD), F32)],
        compiler_params=_params("arbitrary", "arbitrary", "arbitrary"),
        name="attention",
    )(q, k, v)


SSD_PREP_COLS = 512


def _ssd_prep_kernel(x_ref, ng_ref, sc_ref, sh_ref, w2_ref, sz_ref, xbc_ref, dt_ref):
    hb = _modulated_norm(x_ref[0], ng_ref[...], sc_ref[0], sh_ref[0]).astype(BF16)
    for j in range(SSM_INNER // SSD_PREP_COLS):
        c0 = j * SSD_PREP_COLS
        sz_ref[0, :, c0:c0 + SSD_PREP_COLS] = _silu(_dot(hb, w2_ref[:, c0:c0 + SSD_PREP_COLS])).astype(BF16)
    for j in range(SSM_CONV_DIM // SSD_PREP_COLS):
        c0 = j * SSD_PREP_COLS
        w0 = SSM_INNER + c0
        xbc_ref[0, :, c0:c0 + SSD_PREP_COLS] = _dot(hb, w2_ref[:, w0:w0 + SSD_PREP_COLS]).astype(BF16)
    w0 = SSM_INNER + SSM_CONV_DIM
    dt_ref[0] = _dot(hb, w2_ref[:, w0:w0 + LANES])


def _ssd_prep(x, ng, scale, shift, w2, tm):
    b, s, _ = x.shape
    full = lambda a: pl.BlockSpec(a.shape, lambda i, j: (0,) * a.ndim)
    per_batch = pl.BlockSpec((1, 1, D_MODEL), lambda i, j: (i, 0, 0))
    row = lambda w: pl.BlockSpec((1, tm, w), lambda i, j: (i, j, 0))
    return pl.pallas_call(
        _ssd_prep_kernel,
        grid=(b, s // tm),
        in_specs=[row(D_MODEL), full(ng), per_batch, per_batch, full(w2)],
        out_specs=[row(SSM_INNER), row(SSM_CONV_DIM), row(LANES)],
        out_shape=[jax.ShapeDtypeStruct((b, s, SSM_INNER), BF16),
                   jax.ShapeDtypeStruct((b, s, SSM_CONV_DIM), BF16),
                   jax.ShapeDtypeStruct((b, s, LANES), F32)],
        compiler_params=_params("arbitrary", "arbitrary"),
        name="ssd_prep",
    )(x, ng, scale, shift, w2)


CONV_COLS = 512


def _conv_kernel(prev_ref, cur_ref, next_ref, w_ref, b_ref, o_ref):
    j = pl.program_id(1)
    tc = cur_ref.shape[1]
    prev = jnp.where(j > 0, prev_ref[0].astype(F32), 0.0)
    nxt = jnp.where(j < pl.num_programs(1) - 1, next_ref[0].astype(F32), 0.0)
    ext = jnp.concatenate([prev, cur_ref[0].astype(F32), nxt], axis=0)
    n = tc + 2 * CONV_HALO
    acc = jnp.zeros((tc, ext.shape[1]), F32) + b_ref[...]
    for w in range(CONV_WIDTH):
        shift = (CONV_WIDTH // 2 - w) % n
        rolled = ext if shift == 0 else pltpu.roll(ext, shift, 0)
        acc = acc + rolled[CONV_HALO:CONV_HALO + tc] * w_ref[w:w + 1, :]
    o_ref[0] = _silu(acc).astype(BF16)


def _conv(xbc, conv_w, conv_b, tc):
    b, s, _ = xbc.shape
    hb = tc // CONV_HALO
    last = s // CONV_HALO - 1
    return pl.pallas_call(
        _conv_kernel,
        grid=(b, s // tc, SSM_CONV_DIM // CONV_COLS),
        in_specs=[pl.BlockSpec((1, CONV_HALO, CONV_COLS), lambda i, j, c: (i, jnp.maximum(j * hb - 1, 0), c)),
                  pl.BlockSpec((1, tc, CONV_COLS), lambda i, j, c: (i, j, c)),
                  pl.BlockSpec((1, CONV_HALO, CONV_COLS), lambda i, j, c: (i, jnp.minimum((j + 1) * hb, last), c)),
                  pl.BlockSpec((8, CONV_COLS), lambda i, j, c: (0, c)),
                  pl.BlockSpec((1, CONV_COLS), lambda i, j, c: (0, c))],
        out_specs=pl.BlockSpec((1, tc, CONV_COLS), lambda i, j, c: (i, j, c)),
        out_shape=jax.ShapeDtypeStruct((b, s, SSM_CONV_DIM), BF16),
        compiler_params=_params("arbitrary", "arbitrary", "arbitrary"),
        name="conv",
    )(xbc, xbc, xbc, conv_w, conv_b)


HEADS_PER_GROUP = SSM_HEADS // SSM_GROUPS


def _ssd_kernel(x_ref, b_ref, c_ref, dt_ref, sz_ref, par_ref, d_ref, nw_ref, o_ref, st_ref, yacc_ref,
                *, cps, nsteps):
    g = pl.program_id(1)
    ph = pl.program_id(2)
    s = pl.program_id(3)
    fwd = ph == 0
    rows = cps * CHUNK

    @pl.when(s == 0)
    def _():
        st_ref[...] = jnp.zeros(st_ref.shape, F32)

    sidx = jnp.where(fwd, s, nsteps - 1 - s)
    shift = lax.rem(LANES - (ph * SSM_HEADS + g * HEADS_PER_GROUP), LANES)
    par = pltpu.roll(par_ref[...], shift, 1)
    a_coef = -jnp.exp(par[0:1, :])
    dt_bias = par[1:2, :]
    row = lax.broadcasted_iota(jnp.int32, (CHUNK, CHUNK), 0)
    col = lax.broadcasted_iota(jnp.int32, (CHUNK, CHUNK), 1)
    tri_mask = (row - col) * jnp.where(fwd, 1, -1) >= 0
    tri = jnp.where(tri_mask, 1.0, 0.0).astype(BF16)
    low_half = col < SSM_HEADDIM
    erow = lax.broadcasted_iota(jnp.int32, (LANES, SSM_GROUP_WIDTH), 0)
    ecol = lax.broadcasted_iota(jnp.int32, (LANES, SSM_GROUP_WIDTH), 1)
    expand = jnp.where((ecol >= erow * SSM_HEADDIM) & (ecol < (erow + 1) * SSM_HEADDIM), 1.0, 0.0).astype(BF16)
    d_row = jnp.where(fwd, d_ref[0:1, :], d_ref[1:2, :])

    def chunk(j, carry):
        jj = jnp.where(fwd, j, cps - 1 - j)
        r0 = pl.multiple_of(jj * CHUNK, CHUNK)
        xb = x_ref[0, pl.ds(r0, CHUNK), :]
        bm = b_ref[0, pl.ds(r0, CHUNK), :]
        cm = c_ref[0, pl.ds(r0, CHUNK), :]
        dt = _softplus(pltpu.roll(dt_ref[0, pl.ds(r0, CHUNK), :], shift, 1) + dt_bias)
        a = dt * a_coef
        acum = _split_dot(tri, a)
        acum_t = acum.T
        alast = jnp.where(fwd, acum[CHUNK - 1:CHUNK, :], acum[0:1, :])
        ecum = jnp.exp(acum)
        dt_dstate = dt * jnp.exp(alast - acum)
        cdec = jnp.exp(_split_dot(jnp.broadcast_to(alast, (8, LANES)), expand)[0:1, :])
        cb = _dot_nt(cm, bm)
        st = st_ref[...]
        yoff = _dot(cm, st.astype(BF16))
        xs_parts = []
        y_parts = []
        for p in range(HEADS_PER_GROUP // 2):
            h0, h1 = 2 * p, 2 * p + 1
            c0 = p * LANES
            xp = xb[:, c0:c0 + LANES].astype(F32)

            def pair(v):
                return jnp.where(low_half, v[:, h0:h0 + 1], v[:, h1:h1 + 1])

            def decay_mix(h):
                diff = acum[:, h:h + 1] - acum_t[h:h + 1, :]
                return (cb * jnp.exp(jnp.where(tri_mask, diff, NEG_BIG))).astype(BF16)

            xdt = xp * pair(dt)
            xdt_b = xdt.astype(BF16)
            ydiag = jnp.where(low_half, _dot(decay_mix(h0), xdt_b), _dot(decay_mix(h1), xdt_b))
            y_parts.append(ydiag + yoff[:, c0:c0 + LANES] * pair(ecum) + xp * d_row[:, c0:c0 + LANES])
            xs_parts.append((xp * pair(dt_dstate)).astype(BF16))
        xs = jnp.concatenate(xs_parts, axis=1)
        bm_t = bm.astype(F32).T.astype(BF16)
        st_ref[...] = st * cdec + _dot(bm_t, xs)
        y = jnp.concatenate(y_parts, axis=1)
        grow = pl.multiple_of(sidx * rows + r0, CHUNK)

        @pl.when(fwd)
        def _():
            yacc_ref[pl.ds(grow, CHUNK), :] = y

        @pl.when(jnp.logical_not(fwd))
        def _():
            yt = (y + yacc_ref[pl.ds(grow, CHUNK), :]) * sz_ref[0, pl.ds(r0, CHUNK), :].astype(F32)
            o_ref[0, pl.ds(r0, CHUNK), :] = (_rms(yt) * nw_ref[...]).astype(BF16)

        return carry

    lax.fori_loop(0, cps, chunk, 0)


def _ssd(xact, dt, sz, par, dexp, nw, cps):
    b, s, _ = xact.shape
    rows = cps * CHUNK
    nsteps = s // rows
    cur = lambda ph, t: jnp.where(ph == 0, t, nsteps - 1 - t)
    late = lambda ph, t: jnp.where(ph == 0, nsteps - 1, nsteps - 1 - t)
    gw = SSM_GROUP_WIDTH // LANES
    b_blk = SSM_INNER // SSM_STATE
    c_blk = b_blk + SSM_GROUPS
    return pl.pallas_call(
        functools.partial(_ssd_kernel, cps=cps, nsteps=nsteps),
        grid=(b, SSM_GROUPS, 2, nsteps),
        in_specs=[pl.BlockSpec((1, rows, SSM_GROUP_WIDTH), lambda i, g, ph, t: (i, cur(ph, t), g)),
                  pl.BlockSpec((1, rows, SSM_STATE), lambda i, g, ph, t: (i, cur(ph, t), b_blk + g)),
                  pl.BlockSpec((1, rows, SSM_STATE), lambda i, g, ph, t: (i, cur(ph, t), c_blk + g)),
                  pl.BlockSpec((1, rows, LANES), lambda i, g, ph, t: (i, cur(ph, t), 0)),
                  pl.BlockSpec((1, rows, SSM_GROUP_WIDTH), lambda i, g, ph, t: (i, late(ph, t), g)),
                  pl.BlockSpec((8, LANES), lambda i, g, ph, t: (0, 0)),
                  pl.BlockSpec((8, SSM_GROUP_WIDTH), lambda i, g, ph, t: (0, g)),
                  pl.BlockSpec((1, SSM_GROUP_WIDTH), lambda i, g, ph, t: (0, g))],
        out_specs=pl.BlockSpec((1, rows, SSM_GROUP_WIDTH), lambda i, g, ph, t: (i, late(ph, t), g)),
        out_shape=jax.ShapeDtypeStruct((b, s, SSM_INNER), BF16),
        scratch_shapes=[pltpu.VMEM((SSM_STATE, SSM_GROUP_WIDTH), F32),
                        pltpu.VMEM((s, SSM_GROUP_WIDTH), F32)],
        compiler_params=_params("arbitrary", "arbitrary", "arbitrary", "arbitrary"),
        name="ssd",
    )(xact, xact, xact, dt, sz, par, dexp, nw)


def _merge_kernel(x_ref, ng_ref, sc_ref, sh_ref, gt_ref, attn_ref, yn_ref, w3_ref, wpa_ref, wpb_ref, wo_ref,
                  o_ref):
    x = x_ref[0]
    hb = _modulated_norm(x, ng_ref[...], sc_ref[0], sh_ref[0]).astype(BF16)
    gate_a = _dot(hb, w3_ref[:, :MLA_WIDTH])
    a_in = (attn_ref[0].astype(F32) * _silu(gate_a)).astype(BF16)
    branch_a = _dot(a_in, wpa_ref[...])
    branch_b = _dot(yn_ref[0], wpb_ref[...])
    g_a = _dot(hb, w3_ref[:, MLA_WIDTH:MLA_WIDTH + D_MODEL])
    g_b = _dot(hb, w3_ref[:, MLA_WIDTH + D_MODEL:])
    merged = _sigmoid(g_a) * branch_a + _sigmoid(g_b) * branch_b
    out = _dot(merged.astype(BF16), wo_ref[...])
    o_ref[0] = x + gt_ref[0] * out


def _merge(x, ng, scale, shift, gate, attn, yn, w3, wpa, wpb, wo, tm):
    b, s, _ = x.shape
    full = lambda a: pl.BlockSpec(a.shape, lambda i, j: (0,) * a.ndim)
    per_batch = pl.BlockSpec((1, 1, D_MODEL), lambda i, j: (i, 0, 0))
    row = lambda w: pl.BlockSpec((1, tm, w), lambda i, j: (i, j, 0))
    return pl.pallas_call(
        _merge_kernel,
        grid=(b, s // tm),
        in_specs=[row(D_MODEL), full(ng), per_batch, per_batch, per_batch, row(MLA_WIDTH), row(SSM_INNER),
                  full(w3), full(wpa), full(wpb), full(wo)],
        out_specs=row(D_MODEL),
        out_shape=jax.ShapeDtypeStruct((b, s, D_MODEL), F32),
        compiler_params=_params("arbitrary", "arbitrary"),
        name="merge",
    )(x, ng, scale, shift, gate, attn, yn, w3, wpa, wpb, wo)


def _rope_tables(s):
    half = QK_ROPE // 2
    inv_freq = jnp.exp(-math.log(ROPE_THETA) * jnp.arange(half, dtype=F32) / half)
    ang = jnp.arange(s, dtype=F32)[:, None] * inv_freq[None, :]
    cos, sin = jnp.cos(ang), jnp.sin(ang)
    zeros = jnp.zeros((s, LANES - QK_ROPE), F32)
    return (jnp.concatenate([cos, cos, zeros], axis=1), jnp.concatenate([-sin, sin, zeros], axis=1))


def _tail_gain(g):
    return jnp.stack([g[:QK_NOPE], jnp.concatenate([g[QK_NOPE:], jnp.zeros((LANES - QK_ROPE,), F32)])])


def _prep_weights(norm_g, w_ada, b_ada, w_in, q_a_norm, w_q_up, kv_a_norm, w_kv_up, q_norm, k_norm, w_proj_a,
                  conv_w, conv_b, dt_bias_f, dt_bias_b, a_log_f, a_log_b, d_f, d_b, ssm_norm, w_proj_b, w_out):
    o_ga = Q_LORA + KV_LORA + QK_ROPE
    o_z = o_ga + MLA_WIDTH
    o_xbc = o_z + SSM_INNER
    o_dt = o_xbc + SSM_CONV_DIM
    o_gm = o_dt + 2 * SSM_HEADS
    zpad = jnp.zeros((D_MODEL, LANES - QK_ROPE), F32)
    w1 = jnp.concatenate([w_in[:, :o_ga], zpad], axis=1).astype(BF16)
    wq = jnp.pad(w_q_up.reshape(Q_LORA, MLA_HEADS, QK_HEAD), ((0, 0), (0, 0), (0, HEAD_PAD - QK_HEAD)))
    wq = wq.reshape(Q_LORA, MLA_HEADS * HEAD_PAD).astype(BF16)
    wkv = w_kv_up.reshape(KV_LORA, MLA_HEADS, 2, QK_NOPE).transpose(0, 2, 1, 3)
    wkv = wkv.reshape(KV_LORA, 2 * MLA_HEADS * QK_NOPE).astype(BF16)
    w2 = jnp.concatenate([w_in[:, o_z:o_gm], zpad], axis=1).astype(BF16)
    w3 = jnp.concatenate([w_in[:, o_ga:o_z], w_in[:, o_gm:]], axis=1).astype(BF16)
    lane_pad = jnp.zeros((LANES - 2 * SSM_HEADS,), F32)
    par = jnp.zeros((8, LANES), F32)
    par = par.at[0].set(jnp.concatenate([a_log_f, a_log_b, lane_pad]))
    par = par.at[1].set(jnp.concatenate([dt_bias_f, dt_bias_b, lane_pad]))
    dexp = jnp.zeros((8, SSM_INNER), F32)
    dexp = dexp.at[0].set(jnp.repeat(d_f, SSM_HEADDIM)).at[1].set(jnp.repeat(d_b, SSM_HEADDIM))
    return dict(
        ng=norm_g.reshape(1, D_MODEL), w_ada=w_ada.astype(BF16), b_ada=b_ada.reshape(1, 3 * D_MODEL),
        w1=w1, qan=q_a_norm.reshape(1, Q_LORA), wq=wq, kvan=kv_a_norm.reshape(1, KV_LORA), wkv=wkv,
        gq=_tail_gain(q_norm), gk=_tail_gain(k_norm), w2=w2, w3=w3,
        conv_w=jnp.pad(conv_w, ((0, 8 - CONV_WIDTH), (0, 0))), conv_b=conv_b.reshape(1, SSM_CONV_DIM),
        par=par, dexp=dexp, nw=ssm_norm.reshape(1, SSM_INNER),
        wpa=w_proj_a.astype(BF16), wpb=w_proj_b.astype(BF16), wo=w_out.astype(BF16))


def _tiles(s):
    return dict(tm=min(512, s), tq=min(512, s), tk=min(512, s), tc=min(256, s), cps=min(4, s // CHUNK))


def _encoder_layer(x, mod, w, cos, sin):
    b, s, _ = x.shape
    t = _tiles(s)
    shift = mod[:, :D_MODEL].reshape(b, 1, D_MODEL)
    scale = mod[:, D_MODEL:2 * D_MODEL].reshape(b, 1, D_MODEL)
    gate = mod[:, 2 * D_MODEL:].reshape(b, 1, D_MODEL)
    q, k, v = _attn_prep(x, w["ng"], scale, shift, w["w1"], w["qan"], w["wq"], w["kvan"], w["wkv"],
                         w["gq"], w["gk"], cos, sin, t["tm"])
    attn = _attention(q, k, v, t["tq"], t["tk"])
    sz, xbc, dt = _ssd_prep(x, w["ng"], scale, shift, w["w2"], t["tm"])
    xact = _conv(xbc, w["conv_w"], w["conv_b"], t["tc"])
    yn = _ssd(xact, dt, sz, w["par"], w["dexp"], w["nw"], t["cps"])
    return _merge(x, w["ng"], scale, shift, gate, attn, yn, w["w3"], w["wpa"], w["wpb"], w["wo"], t["tm"])


def kernel(x_prompt, x_sample, c_prompt, c_sample, norm_g, w_ada, b_ada, w_in, q_a_norm, w_q_up, kv_a_norm, w_kv_up, q_norm, k_norm, w_proj_a, conv_w, conv_b, dt_bias_f, dt_bias_b, a_log_f, a_log_b, d_f, d_b, ssm_norm, w_proj_b, w_out):
    layer_params = (norm_g, w_ada, b_ada, w_in, q_a_norm, w_q_up, kv_a_norm, w_kv_up, q_norm, k_norm, w_proj_a,
                    conv_w, conv_b, dt_bias_f, dt_bias_b, a_log_f, a_log_b, d_f, d_b, ssm_norm, w_proj_b, w_out)
    nb = x_prompt.shape[0]
    c_all = jnp.concatenate([c_prompt, c_sample], axis=0)
    c_all = jnp.pad(c_all, ((0, -c_all.shape[0] % 8), (0, 0)))
    tables = {s: _rope_tables(s) for s in {x_prompt.shape[1], x_sample.shape[1]}}
    y_prompt, y_sample = x_prompt, x_sample
    for l in range(norm_g.shape[0]):
        w = _prep_weights(*(p[l] for p in layer_params))
        mod = _mod(c_all, w["w_ada"], w["b_ada"])
        y_prompt = _encoder_layer(y_prompt, mod[:nb], w, *tables[y_prompt.shape[1]])
        y_sample = _encoder_layer(y_sample, mod[nb:nb + x_sample.shape[0]], w, *tables[y_sample.shape[1]])
    return (y_prompt, y_sample)
```

```python
import functools
import math

import jax
import jax.numpy as jnp
from jax import lax
from jax.experimental import pallas as pl
from jax.experimental.pallas import tpu as pltpu

D_MODEL = 1024
MLA_HEADS = 8
QK_NOPE = 128
QK_ROPE = 64
QK_HEAD = QK_NOPE + QK_ROPE
V_HEAD = 128
Q_LORA = 384
KV_LORA = 256
MLA_WIDTH = MLA_HEADS * V_HEAD
ROPE_THETA = 10000.0
HEAD_PAD = 256

SSM_INNER = 2048
SSM_HEADDIM = 64
SSM_HEADS = 32
SSM_GROUPS = 4
SSM_STATE = 128
SSM_GROUP_WIDTH = SSM_INNER // SSM_GROUPS
SSM_CONV_DIM = SSM_INNER + 2 * SSM_GROUPS * SSM_STATE
CONV_WIDTH = 5
CHUNK = 128
EPS = 1e-6

LANES = 128
CONV_HALO = 16
VMEM_LIMIT = 56 * 1024 * 1024

F32 = jnp.float32
BF16 = jnp.bfloat16
NEG_BIG = -1e30


def _dot(a, b):
    return jnp.dot(a, b, preferred_element_type=F32)


def _dot_nt(a, b):
    return lax.dot_general(a, b, (((1,), (1,)), ((), ())), preferred_element_type=F32)


def _sigmoid(t):
    return 1.0 / (1.0 + jnp.exp(-t))


def _silu(t):
    return t * _sigmoid(t)


def _softplus(t):
    return jnp.maximum(t, 0.0) + jnp.log1p(jnp.exp(-jnp.abs(t)))


def _rms(t):
    return t * lax.rsqrt(jnp.mean(t * t, axis=-1, keepdims=True) + EPS)


def _modulated_norm(x, g, scale, shift):
    return (_rms(x) * g) * (1.0 + scale) + shift


def _split_dot(m, v):
    hi = v.astype(BF16)
    r1 = v - hi.astype(F32)
    mid = r1.astype(BF16)
    lo = (r1 - mid.astype(F32)).astype(BF16)
    return _dot(m, hi) + _dot(m, mid) + _dot(m, lo)


def _params(*sem):
    return pltpu.CompilerParams(dimension_semantics=sem, vmem_limit_bytes=VMEM_LIMIT)


def _mod_kernel(c_ref, w_ref, b_ref, o_ref):
    o_ref[...] = _dot(_silu(c_ref[...]).astype(BF16), w_ref[...]) + b_ref[...]


def _mod(c, w_ada, b_ada):
    n = c.shape[0]
    return pl.pallas_call(
        _mod_kernel,
        grid=(3,),
        in_specs=[pl.BlockSpec((n, D_MODEL), lambda j: (0, 0)),
                  pl.BlockSpec((D_MODEL, D_MODEL), lambda j: (0, j)),
                  pl.BlockSpec((1, D_MODEL), lambda j: (0, j))],
        out_specs=pl.BlockSpec((n, D_MODEL), lambda j: (0, j)),
        out_shape=jax.ShapeDtypeStruct((n, 3 * D_MODEL), F32),
        compiler_params=_params("arbitrary"),
        name="mod",
    )(c, w_ada, b_ada)


def _attn_prep_kernel(x_ref, ng_ref, sc_ref, sh_ref, w1_ref, qan_ref, wq_ref, kvan_ref, wkv_ref,
                      gq_ref, gk_ref, cos_ref, sin_ref, q_ref, k_ref, v_ref):
    hb = _modulated_norm(x_ref[0], ng_ref[...], sc_ref[0], sh_ref[0]).astype(BF16)
    p1 = _dot(hb, w1_ref[...])
    qn = (_rms(p1[:, :Q_LORA]) * qan_ref[...]).astype(BF16)
    qa = _dot(qn, wq_ref[...])
    kvn = (_rms(p1[:, Q_LORA:Q_LORA + KV_LORA]) * kvan_ref[...]).astype(BF16)
    kv = _dot(kvn, wkv_ref[...])
    kr = p1[:, Q_LORA + KV_LORA:]
    cos = cos_ref[...]
    sin = sin_ref[...]
    lane = lax.broadcasted_iota(jnp.int32, cos.shape, 1)
    first_half = lane < QK_ROPE // 2

    def rope(t):
        swapped = jnp.where(first_half, pltpu.roll(t, LANES - QK_ROPE // 2, 1),
                            pltpu.roll(t, QK_ROPE // 2, 1))
        return t * cos + swapped * sin

    gq_nope, gq_tail = gq_ref[0:1, :], gq_ref[1:2, :]
    gk_nope, gk_tail = gk_ref[0:1, :], gk_ref[1:2, :]
    kr_rot = rope(kr * gk_tail)
    kr_ss = jnp.sum(kr * kr, axis=-1, keepdims=True)
    scale = QK_HEAD ** -0.5
    for h in range(MLA_HEADS):
        c0 = h * HEAD_PAD
        q_nope = qa[:, c0:c0 + QK_NOPE]
        q_tail = qa[:, c0 + QK_NOPE:c0 + HEAD_PAD]
        ss = jnp.sum(q_nope * q_nope, axis=-1, keepdims=True) + jnp.sum(q_tail * q_tail, axis=-1, keepdims=True)
        r = lax.rsqrt(ss * (1.0 / QK_HEAD) + EPS) * scale
        q_ref[0, :, c0:c0 + QK_NOPE] = (q_nope * gq_nope * r).astype(BF16)
        q_ref[0, :, c0 + QK_NOPE:c0 + HEAD_PAD] = (rope(q_tail * gq_tail) * r).astype(BF16)
        k_nope = kv[:, h * QK_NOPE:(h + 1) * QK_NOPE]
        ssk = jnp.sum(k_nope * k_nope, axis=-1, keepdims=True) + kr_ss
        rk = lax.rsqrt(ssk * (1.0 / QK_HEAD) + EPS)
        k_ref[0, :, c0:c0 + QK_NOPE] = (k_nope * gk_nope * rk).astype(BF16)
        k_ref[0, :, c0 + QK_NOPE:c0 + HEAD_PAD] = (kr_rot * rk).astype(BF16)
        v0 = MLA_HEADS * QK_NOPE + h * V_HEAD
        v_ref[0, :, 2 * h * V_HEAD:(2 * h + 1) * V_HEAD] = kv[:, v0:v0 + V_HEAD].astype(BF16)
        v_ref[0, :, (2 * h + 1) * V_HEAD:(2 * h + 2) * V_HEAD] = jnp.ones((kv.shape[0], V_HEAD), BF16)


def _attn_prep(x, ng, scale, shift, w1, qan, wq, kvan, wkv, gq, gk, cos, sin, tm):
    b, s, _ = x.shape
    full = lambda a: pl.BlockSpec(a.shape, lambda i, j: (0,) * a.ndim)
    per_batch = pl.BlockSpec((1, 1, D_MODEL), lambda i, j: (i, 0, 0))
    return pl.pallas_call(
        _attn_prep_kernel,
        grid=(b, s // tm),
        in_specs=[pl.BlockSpec((1, tm, D_MODEL), lambda i, j: (i, j, 0)),
                  full(ng), per_batch, per_batch, full(w1), full(qan), full(wq), full(kvan), full(wkv),
                  full(gq), full(gk),
                  pl.BlockSpec((tm, LANES), lambda i, j: (j, 0)),
                  pl.BlockSpec((tm, LANES), lambda i, j: (j, 0))],
        out_specs=[pl.BlockSpec((1, tm, MLA_HEADS * HEAD_PAD), lambda i, j: (i, j, 0)),
                   pl.BlockSpec((1, tm, MLA_HEADS * HEAD_PAD), lambda i, j: (i, j, 0)),
                   pl.BlockSpec((1, tm, 2 * MLA_WIDTH), lambda i, j: (i, j, 0))],
        out_shape=[jax.ShapeDtypeStruct((b, s, MLA_HEADS * HEAD_PAD), BF16),
                   jax.ShapeDtypeStruct((b, s, MLA_HEADS * HEAD_PAD), BF16),
                   jax.ShapeDtypeStruct((b, s, 2 * MLA_WIDTH), BF16)],
        compiler_params=_params("arbitrary", "arbitrary"),
        name="attn_prep",
    )(x, ng, scale, shift, w1, qan, wq, kvan, wkv, gq, gk, cos, sin)


ATTN_UNROLL = 4


def _attention_kernel(q_ref, k_ref, v_ref, o_ref, m_ref, acc_ref, *, tk, unroll):
    m_ref[...] = jnp.full(m_ref.shape, NEG_BIG, F32)
    acc_ref[...] = jnp.zeros(acc_ref.shape, F32)
    lane_tiles = tk // LANES

    def body(j, carry):
        r0 = pl.multiple_of(j * tk, tk)
        s = _dot_nt(q_ref[0], k_ref[0, pl.ds(r0, tk), :])
        tiles = [s[:, c * LANES:(c + 1) * LANES] for c in range(lane_tiles)]
        tile_max = functools.reduce(jnp.maximum, tiles)
        m_old = m_ref[...]
        m_new = jnp.maximum(m_old, jnp.max(tile_max, axis=1, keepdims=True))
        alpha = jnp.exp(m_old - m_new)
        p = jnp.concatenate([jnp.exp(t - m_new).astype(BF16) for t in tiles], axis=1)
        pv = _dot(p, v_ref[0, pl.ds(r0, tk), :])
        acc_ref[:, :V_HEAD] = alpha * acc_ref[:, :V_HEAD] + pv[:, :V_HEAD]
        acc_ref[:, V_HEAD:] = alpha * acc_ref[:, V_HEAD:] + pv[:, V_HEAD:]
        m_ref[...] = m_new
        return carry

    lax.fori_loop(0, k_ref.shape[1] // tk, body, 0, unroll=unroll)
    o_ref[0] = (acc_ref[:, :V_HEAD] / acc_ref[:, V_HEAD:]).astype(BF16)


def _attention(q, k, v, tq, tk):
    b, s, _ = q.shape
    return pl.pallas_call(
        functools.partial(_attention_kernel, tk=tk, unroll=min(ATTN_UNROLL, s // tk)),
        grid=(b, MLA_HEADS, s // tq),
        in_specs=[pl.BlockSpec((1, tq, HEAD_PAD), lambda i, h, j: (i, j, h)),
                  pl.BlockSpec((1, s, HEAD_PAD), lambda i, h, j: (i, 0, h)),
                  pl.BlockSpec((1, s, 2 * V_HEAD), lambda i, h, j: (i, 0, h))],
        out_specs=pl.BlockSpec((1, tq, V_HEAD), lambda i, h, j: (i, j, h)),
        out_shape=jax.ShapeDtypeStruct((b, s, MLA_WIDTH), BF16),
        scratch_shapes=[pltpu.VMEM((tq, LANES), F32), pltpu.VMEM((tq, 2 * V_HEAD), F32)],
        compiler_params=_params("arbitrary", "arbitrary", "arbitrary"),
        name="attention",
    )(q, k, v)


SSD_PREP_COLS = 512


def _ssd_prep_kernel(x_ref, ng_ref, sc_ref, sh_ref, w2_ref, sz_ref, xbc_ref, dt_ref):
    hb = _modulated_norm(x_ref[0], ng_ref[...], sc_ref[0], sh_ref[0]).astype(BF16)
    for j in range(SSM_INNER // SSD_PREP_COLS):
        c0 = j * SSD_PREP_COLS
        sz_ref[0, :, c0:c0 + SSD_PREP_COLS] = _silu(_dot(hb, w2_ref[:, c0:c0 + SSD_PREP_COLS])).astype(BF16)
    for j in range(SSM_CONV_DIM // SSD_PREP_COLS):
        c0 = j * SSD_PREP_COLS
        w0 = SSM_INNER + c0
        xbc_ref[0, :, c0:c0 + SSD_PREP_COLS] = _dot(hb, w2_ref[:, w0:w0 + SSD_PREP_COLS]).astype(BF16)
    w0 = SSM_INNER + SSM_CONV_DIM
    dt_ref[0] = _dot(hb, w2_ref[:, w0:w0 + LANES])


def _ssd_prep(x, ng, scale, shift, w2, tm):
    b, s, _ = x.shape
    full = lambda a: pl.BlockSpec(a.shape, lambda i, j: (0,) * a.ndim)
    per_batch = pl.BlockSpec((1, 1, D_MODEL), lambda i, j: (i, 0, 0))
    row = lambda w: pl.BlockSpec((1, tm, w), lambda i, j: (i, j, 0))
    return pl.pallas_call(
        _ssd_prep_kernel,
        grid=(b, s // tm),
        in_specs=[row(D_MODEL), full(ng), per_batch, per_batch, full(w2)],
        out_specs=[row(SSM_INNER), row(SSM_CONV_DIM), row(LANES)],
        out_shape=[jax.ShapeDtypeStruct((b, s, SSM_INNER), BF16),
                   jax.ShapeDtypeStruct((b, s, SSM_CONV_DIM), BF16),
                   jax.ShapeDtypeStruct((b, s, LANES), F32)],
        compiler_params=_params("arbitrary", "arbitrary"),
        name="ssd_prep",
    )(x, ng, scale, shift, w2)


CONV_COLS = 512


def _conv_kernel(prev_ref, cur_ref, next_ref, w_ref, b_ref, o_ref):
    j = pl.program_id(1)
    tc = cur_ref.shape[1]
    prev = jnp.where(j > 0, prev_ref[0].astype(F32), 0.0)
    nxt = jnp.where(j < pl.num_programs(1) - 1, next_ref[0].astype(F32), 0.0)
    ext = jnp.concatenate([prev, cur_ref[0].astype(F32), nxt], axis=0)
    n = tc + 2 * CONV_HALO
    acc = jnp.zeros((tc, ext.shape[1]), F32) + b_ref[...]
    for w in range(CONV_WIDTH):
        shift = (CONV_WIDTH // 2 - w) % n
        rolled = ext if shift == 0 else pltpu.roll(ext, shift, 0)
        acc = acc + rolled[CONV_HALO:CONV_HALO + tc] * w_ref[w:w + 1, :]
    o_ref[0] = _silu(acc).astype(BF16)


def _conv(xbc, conv_w, conv_b, tc):
    b, s, _ = xbc.shape
    hb = tc // CONV_HALO
    last = s // CONV_HALO - 1
    return pl.pallas_call(
        _conv_kernel,
        grid=(b, s // tc, SSM_CONV_DIM // CONV_COLS),
        in_specs=[pl.BlockSpec((1, CONV_HALO, CONV_COLS), lambda i, j, c: (i, jnp.maximum(j * hb - 1, 0), c)),
                  pl.BlockSpec((1, tc, CONV_COLS), lambda i, j, c: (i, j, c)),
                  pl.BlockSpec((1, CONV_HALO, CONV_COLS), lambda i, j, c: (i, jnp.minimum((j + 1) * hb, last), c)),
                  pl.BlockSpec((8, CONV_COLS), lambda i, j, c: (0, c)),
                  pl.BlockSpec((1, CONV_COLS), lambda i, j, c: (0, c))],
        out_specs=pl.BlockSpec((1, tc, CONV_COLS), lambda i, j, c: (i, j, c)),
        out_shape=jax.ShapeDtypeStruct((b, s, SSM_CONV_DIM), BF16),
        compiler_params=_params("arbitrary", "arbitrary", "arbitrary"),
        name="conv",
    )(xbc, xbc, xbc, conv_w, conv_b)


HEADS_PER_GROUP = SSM_HEADS // SSM_GROUPS


def _ssd_kernel(x_ref, b_ref, c_ref, dt_ref, sz_ref, par_ref, d_ref, nw_ref, o_ref, st_ref, yacc_ref,
                *, cps, nsteps):
    g = pl.program_id(1)
    ph = pl.program_id(2)
    s = pl.program_id(3)
    fwd = ph == 0
    rows = cps * CHUNK

    @pl.when(s == 0)
    def _():
        st_ref[...] = jnp.zeros(st_ref.shape, F32)

    sidx = jnp.where(fwd, s, nsteps - 1 - s)
    shift = lax.rem(LANES - (ph * SSM_HEADS + g * HEADS_PER_GROUP), LANES)
    par = pltpu.roll(par_ref[...], shift, 1)
    a_coef = -jnp.exp(par[0:1, :])
    dt_bias = par[1:2, :]
    row = lax.broadcasted_iota(jnp.int32, (CHUNK, CHUNK), 0)
    col = lax.broadcasted_iota(jnp.int32, (CHUNK, CHUNK), 1)
    tri_mask = (row - col) * jnp.where(fwd, 1, -1) >= 0
    tri = jnp.where(tri_mask, 1.0, 0.0).astype(BF16)
    low_half = col < SSM_HEADDIM
    erow = lax.broadcasted_iota(jnp.int32, (LANES, SSM_GROUP_WIDTH), 0)
    ecol = lax.broadcasted_iota(jnp.int32, (LANES, SSM_GROUP_WIDTH), 1)
    expand = jnp.where((ecol >= erow * SSM_HEADDIM) & (ecol < (erow + 1) * SSM_HEADDIM), 1.0, 0.0).astype(BF16)
    d_row = jnp.where(fwd, d_ref[0:1, :], d_ref[1:2, :])

    def chunk(j, carry):
        jj = jnp.where(fwd, j, cps - 1 - j)
        r0 = pl.multiple_of(jj * CHUNK, CHUNK)
        xb = x_ref[0, pl.ds(r0, CHUNK), :]
        bm = b_ref[0, pl.ds(r0, CHUNK), :]
        cm = c_ref[0, pl.ds(r0, CHUNK), :]
        dt = _softplus(pltpu.roll(dt_ref[0, pl.ds(r0, CHUNK), :], shift, 1) + dt_bias)
        a = dt * a_coef
        acum = _split_dot(tri, a)
        acum_t = acum.T
        alast = jnp.where(fwd, acum[CHUNK - 1:CHUNK, :], acum[0:1, :])
        ecum = jnp.exp(acum)
        dt_dstate = dt * jnp.exp(alast - acum)
        cdec = jnp.exp(_split_dot(jnp.broadcast_to(alast, (8, LANES)), expand)[0:1, :])
        cb = _dot_nt(cm, bm)
        st = st_ref[...]
        yoff = _dot(cm, st.astype(BF16))
        xs_parts = []
        y_parts = []
        for p in range(HEADS_PER_GROUP // 2):
            h0, h1 = 2 * p, 2 * p + 1
            c0 = p * LANES
            xp = xb[:, c0:c0 + LANES].astype(F32)

            def pair(v):
                return jnp.where(low_half, v[:, h0:h0 + 1], v[:, h1:h1 + 1])

            def decay_mix(h):
                diff = acum[:, h:h + 1] - acum_t[h:h + 1, :]
                return (cb * jnp.exp(jnp.where(tri_mask, diff, NEG_BIG))).astype(BF16)

            xdt = xp * pair(dt)
            xdt_b = xdt.astype(BF16)
            ydiag = jnp.where(low_half, _dot(decay_mix(h0), xdt_b), _dot(decay_mix(h1), xdt_b))
            y_parts.append(ydiag + yoff[:, c0:c0 + LANES] * pair(ecum) + xp * d_row[:, c0:c0 + LANES])
            xs_parts.append((xp * pair(dt_dstate)).astype(BF16))
        xs = jnp.concatenate(xs_parts, axis=1)
        bm_t = bm.astype(F32).T.astype(BF16)
        st_ref[...] = st * cdec + _dot(bm_t, xs)
        y = jnp.concatenate(y_parts, axis=1)
        grow = pl.multiple_of(sidx * rows + r0, CHUNK)

        @pl.when(fwd)
        def _():
            yacc_ref[pl.ds(grow, CHUNK), :] = y

        @pl.when(jnp.logical_not(fwd))
        def _():
            yt = (y + yacc_ref[pl.ds(grow, CHUNK), :]) * sz_ref[0, pl.ds(r0, CHUNK), :].astype(F32)
            o_ref[0, pl.ds(r0, CHUNK), :] = (_rms(yt) * nw_ref[...]).astype(BF16)

        return carry

    lax.fori_loop(0, cps, chunk, 0)


def _ssd(xact, dt, sz, par, dexp, nw, cps):
    b, s, _ = xact.shape
    rows = cps * CHUNK
    nsteps = s // rows
    cur = lambda ph, t: jnp.where(ph == 0, t, nsteps - 1 - t)
    late = lambda ph, t: jnp.where(ph == 0, nsteps - 1, nsteps - 1 - t)
    gw = SSM_GROUP_WIDTH // LANES
    b_blk = SSM_INNER // SSM_STATE
    c_blk = b_blk + SSM_GROUPS
    return pl.pallas_call(
        functools.partial(_ssd_kernel, cps=cps, nsteps=nsteps),
        grid=(b, SSM_GROUPS, 2, nsteps),
        in_specs=[pl.BlockSpec((1, rows, SSM_GROUP_WIDTH), lambda i, g, ph, t: (i, cur(ph, t), g)),
                  pl.BlockSpec((1, rows, SSM_STATE), lambda i, g, ph, t: (i, cur(ph, t), b_blk + g)),
                  pl.BlockSpec((1, rows, SSM_STATE), lambda i, g, ph, t: (i, cur(ph, t), c_blk + g)),
                  pl.BlockSpec((1, rows, LANES), lambda i, g, ph, t: (i, cur(ph, t), 0)),
                  pl.BlockSpec((1, rows, SSM_GROUP_WIDTH), lambda i, g, ph, t: (i, late(ph, t), g)),
                  pl.BlockSpec((8, LANES), lambda i, g, ph, t: (0, 0)),
                  pl.BlockSpec((8, SSM_GROUP_WIDTH), lambda i, g, ph, t: (0, g)),
                  pl.BlockSpec((1, SSM_GROUP_WIDTH), lambda i, g, ph, t: (0, g))],
        out_specs=pl.BlockSpec((1, rows, SSM_GROUP_WIDTH), lambda i, g, ph, t: (i, late(ph, t), g)),
        out_shape=jax.ShapeDtypeStruct((b, s, SSM_INNER), BF16),
        scratch_shapes=[pltpu.VMEM((SSM_STATE, SSM_GROUP_WIDTH), F32),
                        pltpu.VMEM((s, SSM_GROUP_WIDTH), F32)],
        compiler_params=_params("arbitrary", "arbitrary", "arbitrary", "arbitrary"),
        name="ssd",
    )(xact, xact, xact, dt, sz, par, dexp, nw)


def _merge_kernel(x_ref, ng_ref, sc_ref, sh_ref, gt_ref, attn_ref, yn_ref, w3_ref, wpa_ref, wpb_ref, wo_ref,
                  o_ref):
    x = x_ref[0]
    hb = _modulated_norm(x, ng_ref[...], sc_ref[0], sh_ref[0]).astype(BF16)
    gate_a = _dot(hb, w3_ref[:, :MLA_WIDTH])
    a_in = (attn_ref[0].astype(F32) * _silu(gate_a)).astype(BF16)
    branch_a = _dot(a_in, wpa_ref[...])
    branch_b = _dot(yn_ref[0], wpb_ref[...])
    g_a = _dot(hb, w3_ref[:, MLA_WIDTH:MLA_WIDTH + D_MODEL])
    g_b = _dot(hb, w3_ref[:, MLA_WIDTH + D_MODEL:])
    merged = _sigmoid(g_a) * branch_a + _sigmoid(g_b) * branch_b
    out = _dot(merged.astype(BF16), wo_ref[...])
    o_ref[0] = x + gt_ref[0] * out


def _merge(x, ng, scale, shift, gate, attn, yn, w3, wpa, wpb, wo, tm):
    b, s, _ = x.shape
    full = lambda a: pl.BlockSpec(a.shape, lambda i, j: (0,) * a.ndim)
    per_batch = pl.BlockSpec((1, 1, D_MODEL), lambda i, j: (i, 0, 0))
    row = lambda w: pl.BlockSpec((1, tm, w), lambda i, j: (i, j, 0))
    return pl.pallas_call(
        _merge_kernel,
        grid=(b, s // tm),
        in_specs=[row(D_MODEL), full(ng), per_batch, per_batch, per_batch, row(MLA_WIDTH), row(SSM_INNER),
                  full(w3), full(wpa), full(wpb), full(wo)],
        out_specs=row(D_MODEL),
        out_shape=jax.ShapeDtypeStruct((b, s, D_MODEL), F32),
        compiler_params=_params("arbitrary", "arbitrary"),
        name="merge",
    )(x, ng, scale, shift, gate, attn, yn, w3, wpa, wpb, wo)


def _rope_tables(s):
    half = QK_ROPE // 2
    inv_freq = jnp.exp(-math.log(ROPE_THETA) * jnp.arange(half, dtype=F32) / half)
    ang = jnp.arange(s, dtype=F32)[:, None] * inv_freq[None, :]
    cos, sin = jnp.cos(ang), jnp.sin(ang)
    zeros = jnp.zeros((s, LANES - QK_ROPE), F32)
    return (jnp.concatenate([cos, cos, zeros], axis=1), jnp.concatenate([-sin, sin, zeros], axis=1))


def _tail_gain(g):
    return jnp.stack([g[:QK_NOPE], jnp.concatenate([g[QK_NOPE:], jnp.zeros((LANES - QK_ROPE,), F32)])])


def _prep_weights(norm_g, w_ada, b_ada, w_in, q_a_norm, w_q_up, kv_a_norm, w_kv_up, q_norm, k_norm, w_proj_a,
                  conv_w, conv_b, dt_bias_f, dt_bias_b, a_log_f, a_log_b, d_f, d_b, ssm_norm, w_proj_b, w_out):
    o_ga = Q_LORA + KV_LORA + QK_ROPE
    o_z = o_ga + MLA_WIDTH
    o_xbc = o_z + SSM_INNER
    o_dt = o_xbc + SSM_CONV_DIM
    o_gm = o_dt + 2 * SSM_HEADS
    zpad = jnp.zeros((D_MODEL, LANES - QK_ROPE), F32)
    w1 = jnp.concatenate([w_in[:, :o_ga], zpad], axis=1).astype(BF16)
    wq = jnp.pad(w_q_up.reshape(Q_LORA, MLA_HEADS, QK_HEAD), ((0, 0), (0, 0), (0, HEAD_PAD - QK_HEAD)))
    wq = wq.reshape(Q_LORA, MLA_HEADS * HEAD_PAD).astype(BF16)
    wkv = w_kv_up.reshape(KV_LORA, MLA_HEADS, 2, QK_NOPE).transpose(0, 2, 1, 3)
    wkv = wkv.reshape(KV_LORA, 2 * MLA_HEADS * QK_NOPE).astype(BF16)
    w2 = jnp.concatenate([w_in[:, o_z:o_gm], zpad], axis=1).astype(BF16)
    w3 = jnp.concatenate([w_in[:, o_ga:o_z], w_in[:, o_gm:]], axis=1).astype(BF16)
    lane_pad = jnp.zeros((LANES - 2 * SSM_HEADS,), F32)
    par = jnp.zeros((8, LANES), F32)
    par = par.at[0].set(jnp.concatenate([a_log_f, a_log_b, lane_pad]))
    par = par.at[1].set(jnp.concatenate([dt_bias_f, dt_bias_b, lane_pad]))
    dexp = jnp.zeros((8, SSM_INNER), F32)
    dexp = dexp.at[0].set(jnp.repeat(d_f, SSM_HEADDIM)).at[1].set(jnp.repeat(d_b, SSM_HEADDIM))
    return dict(
        ng=norm_g.reshape(1, D_MODEL), w_ada=w_ada.astype(BF16), b_ada=b_ada.reshape(1, 3 * D_MODEL),
        w1=w1, qan=q_a_norm.reshape(1, Q_LORA), wq=wq, kvan=kv_a_norm.reshape(1, KV_LORA), wkv=wkv,
        gq=_tail_gain(q_norm), gk=_tail_gain(k_norm), w2=w2, w3=w3,
        conv_w=jnp.pad(conv_w, ((0, 8 - CONV_WIDTH), (0, 0))), conv_b=conv_b.reshape(1, SSM_CONV_DIM),
        par=par, dexp=dexp, nw=ssm_norm.reshape(1, SSM_INNER),
        wpa=w_proj_a.astype(BF16), wpb=w_proj_b.astype(BF16), wo=w_out.astype(BF16))


def _tiles(s):
    return dict(tm=min(512, s), tq=min(512, s), tk=min(1024, s), tc=min(256, s), cps=min(4, s // CHUNK))


def _encoder_layer(x, mod, w, cos, sin):
    b, s, _ = x.shape
    t = _tiles(s)
    shift = mod[:, :D_MODEL].reshape(b, 1, D_MODEL)
    scale = mod[:, D_MODEL:2 * D_MODEL].reshape(b, 1, D_MODEL)
    gate = mod[:, 2 * D_MODEL:].reshape(b, 1, D_MODEL)
    q, k, v = _attn_prep(x, w["ng"], scale, shift, w["w1"], w["qan"], w["wq"], w["kvan"], w["wkv"],
                         w["gq"], w["gk"], cos, sin, t["tm"])
    attn = _attention(q, k, v, t["tq"], t["tk"])
    sz, xbc, dt = _ssd_prep(x, w["ng"], scale, shift, w["w2"], t["tm"])
    xact = _conv(xbc, w["conv_w"], w["conv_b"], t["tc"])
    yn = _ssd(xact, dt, sz, w["par"], w["dexp"], w["nw"], t["cps"])
    return _merge(x, w["ng"], scale, shift, gate, attn, yn, w["w3"], w["wpa"], w["wpb"], w["wo"], t["tm"])


def kernel(x_prompt, x_sample, c_prompt, c_sample, norm_g, w_ada, b_ada, w_in, q_a_norm, w_q_up, kv_a_norm, w_kv_up, q_norm, k_norm, w_proj_a, conv_w, conv_b, dt_bias_f, dt_bias_b, a_log_f, a_log_b, d_f, d_b, ssm_norm, w_proj_b, w_out):
    layer_params = (norm_g, w_ada, b_ada, w_in, q_a_norm, w_q_up, kv_a_norm, w_kv_up, q_norm, k_norm, w_proj_a,
                    conv_w, conv_b, dt_bias_f, dt_bias_b, a_log_f, a_log_b, d_f, d_b, ssm_norm, w_proj_b, w_out)
    nb = x_prompt.shape[0]
    c_all = jnp.concatenate([c_prompt, c_sample], axis=0)
    c_all = jnp.pad(c_all, ((0, -c_all.shape[0] % 8), (0, 0)))
    tables = {s: _rope_tables(s) for s in {x_prompt.shape[1], x_sample.shape[1]}}
    y_prompt, y_sample = x_prompt, x_sample
    for l in range(norm_g.shape[0]):
        w = _prep_weights(*(p[l] for p in layer_params))
        mod = _mod(c_all, w["w_ada"], w["b_ada"])
        y_prompt = _encoder_layer(y_prompt, mod[:nb], w, *tables[y_prompt.shape[1]])
        y_sample = _encoder_layer(y_sample, mod[nb:nb + x_sample.shape[0]], w, *tables[y_sample.shape[1]])
    return (y_prompt, y_sample)
```

```python
import functools
import math

import jax
import jax.numpy as jnp
from jax import lax
from jax.experimental import pallas as pl
from jax.experimental.pallas import tpu as pltpu

D_MODEL = 1024
MLA_HEADS = 8
QK_NOPE = 128
QK_ROPE = 64
QK_HEAD = QK_NOPE + QK_ROPE
V_HEAD = 128
Q_LORA = 384
KV_LORA = 256
MLA_WIDTH = MLA_HEADS * V_HEAD
ROPE_THETA = 10000.0
HEAD_PAD = 256

SSM_INNER = 2048
SSM_HEADDIM = 64
SSM_HEADS = 32
SSM_GROUPS = 4
SSM_STATE = 128
SSM_GROUP_WIDTH = SSM_INNER // SSM_GROUPS
SSM_CONV_DIM = SSM_INNER + 2 * SSM_GROUPS * SSM_STATE
CONV_WIDTH = 5
CHUNK = 128
EPS = 1e-6

LANES = 128
CONV_HALO = 16
VMEM_LIMIT = 56 * 1024 * 1024

F32 = jnp.float32
BF16 = jnp.bfloat16
NEG_BIG = -1e30
LOG2_E = 1.4426950408889634


def _dot(a, b):
    return jnp.dot(a, b, preferred_element_type=F32)


def _dot_nt(a, b):
    return lax.dot_general(a, b, (((1,), (1,)), ((), ())), preferred_element_type=F32)


def _sigmoid(t):
    return 1.0 / (1.0 + jnp.exp2(t * (-LOG2_E)))


def _silu(t):
    return t * _sigmoid(t)


def _softplus(t):
    return jnp.maximum(t, 0.0) + jnp.log1p(jnp.exp(-jnp.abs(t)))


def _rms(t):
    return t * lax.rsqrt(jnp.mean(t * t, axis=-1, keepdims=True) + EPS)


def _modulated_norm(x, g, scale, shift):
    return (_rms(x) * g) * (1.0 + scale) + shift


def _split_dot(m, v):
    hi = v.astype(BF16)
    r1 = v - hi.astype(F32)
    mid = r1.astype(BF16)
    lo = (r1 - mid.astype(F32)).astype(BF16)
    return _dot(m, hi) + _dot(m, mid) + _dot(m, lo)


def _params(*sem):
    return pltpu.CompilerParams(dimension_semantics=sem, vmem_limit_bytes=VMEM_LIMIT)


def _mod_kernel(c_ref, w_ref, b_ref, o_ref):
    o_ref[...] = _dot(_silu(c_ref[...]).astype(BF16), w_ref[...]) + b_ref[...]


def _mod(c, w_ada, b_ada):
    n = c.shape[0]
    return pl.pallas_call(
        _mod_kernel,
        grid=(3,),
        in_specs=[pl.BlockSpec((n, D_MODEL), lambda j: (0, 0)),
                  pl.BlockSpec((D_MODEL, D_MODEL), lambda j: (0, j)),
                  pl.BlockSpec((1, D_MODEL), lambda j: (0, j))],
        out_specs=pl.BlockSpec((n, D_MODEL), lambda j: (0, j)),
        out_shape=jax.ShapeDtypeStruct((n, 3 * D_MODEL), F32),
        compiler_params=_params("arbitrary"),
        name="mod",
    )(c, w_ada, b_ada)


def _attn_prep_kernel(x_ref, ng_ref, sc_ref, sh_ref, w1_ref, qan_ref, wq_ref, kvan_ref, wkv_ref,
                      gq_ref, gk_ref, cos_ref, sin_ref, q_ref, k_ref, v_ref):
    hb = _modulated_norm(x_ref[0], ng_ref[...], sc_ref[0], sh_ref[0]).astype(BF16)
    p1 = _dot(hb, w1_ref[...])
    qn = (_rms(p1[:, :Q_LORA]) * qan_ref[...]).astype(BF16)
    qa = _dot(qn, wq_ref[...])
    kvn = (_rms(p1[:, Q_LORA:Q_LORA + KV_LORA]) * kvan_ref[...]).astype(BF16)
    kv = _dot(kvn, wkv_ref[...])
    kr = p1[:, Q_LORA + KV_LORA:]
    cos = cos_ref[...]
    sin = sin_ref[...]
    lane = lax.broadcasted_iota(jnp.int32, cos.shape, 1)
    first_half = lane < QK_ROPE // 2

    def rope(t):
        swapped = jnp.where(first_half, pltpu.roll(t, LANES - QK_ROPE // 2, 1),
                            pltpu.roll(t, QK_ROPE // 2, 1))
        return t * cos + swapped * sin

    gq_nope, gq_tail = gq_ref[0:1, :], gq_ref[1:2, :]
    gk_nope, gk_tail = gk_ref[0:1, :], gk_ref[1:2, :]
    kr_rot = rope(kr * gk_tail)
    kr_ss = jnp.sum(kr * kr, axis=-1, keepdims=True)
    scale = QK_HEAD ** -0.5 * LOG2_E
    for h in range(MLA_HEADS):
        c0 = h * HEAD_PAD
        q_nope = qa[:, c0:c0 + QK_NOPE]
        q_tail = qa[:, c0 + QK_NOPE:c0 + HEAD_PAD]
        ss = jnp.sum(q_nope * q_nope, axis=-1, keepdims=True) + jnp.sum(q_tail * q_tail, axis=-1, keepdims=True)
        r = lax.rsqrt(ss * (1.0 / QK_HEAD) + EPS) * scale
        q_ref[0, :, c0:c0 + QK_NOPE] = (q_nope * gq_nope * r).astype(BF16)
        q_ref[0, :, c0 + QK_NOPE:c0 + HEAD_PAD] = (rope(q_tail * gq_tail) * r).astype(BF16)
        k_nope = kv[:, h * QK_NOPE:(h + 1) * QK_NOPE]
        ssk = jnp.sum(k_nope * k_nope, axis=-1, keepdims=True) + kr_ss
        rk = lax.rsqrt(ssk * (1.0 / QK_HEAD) + EPS)
        k_ref[0, :, c0:c0 + QK_NOPE] = (k_nope * gk_nope * rk).astype(BF16)
        k_ref[0, :, c0 + QK_NOPE:c0 + HEAD_PAD] = (kr_rot * rk).astype(BF16)
        v0 = MLA_HEADS * QK_NOPE + h * V_HEAD
        v_ref[0, :, 2 * h * V_HEAD:(2 * h + 1) * V_HEAD] = kv[:, v0:v0 + V_HEAD].astype(BF16)
        v_ref[0, :, (2 * h + 1) * V_HEAD:(2 * h + 2) * V_HEAD] = jnp.ones((kv.shape[0], V_HEAD), BF16)


def _attn_prep(x, ng, scale, shift, w1, qan, wq, kvan, wkv, gq, gk, cos, sin, tm):
    b, s, _ = x.shape
    full = lambda a: pl.BlockSpec(a.shape, lambda i, j: (0,) * a.ndim)
    per_batch = pl.BlockSpec((1, 1, D_MODEL), lambda i, j: (i, 0, 0))
    return pl.pallas_call(
        _attn_prep_kernel,
        grid=(b, s // tm),
        in_specs=[pl.BlockSpec((1, tm, D_MODEL), lambda i, j: (i, j, 0)),
                  full(ng), per_batch, per_batch, full(w1), full(qan), full(wq), full(kvan), full(wkv),
                  full(gq), full(gk),
                  pl.BlockSpec((tm, LANES), lambda i, j: (j, 0)),
                  pl.BlockSpec((tm, LANES), lambda i, j: (j, 0))],
        out_specs=[pl.BlockSpec((1, tm, MLA_HEADS * HEAD_PAD), lambda i, j: (i, j, 0)),
                   pl.BlockSpec((1, tm, MLA_HEADS * HEAD_PAD), lambda i, j: (i, j, 0)),
                   pl.BlockSpec((1, tm, 2 * MLA_WIDTH), lambda i, j: (i, j, 0))],
        out_shape=[jax.ShapeDtypeStruct((b, s, MLA_HEADS * HEAD_PAD), BF16),
                   jax.ShapeDtypeStruct((b, s, MLA_HEADS * HEAD_PAD), BF16),
                   jax.ShapeDtypeStruct((b, s, 2 * MLA_WIDTH), BF16)],
        compiler_params=_params("arbitrary", "arbitrary"),
        name="attn_prep",
    )(x, ng, scale, shift, w1, qan, wq, kvan, wkv, gq, gk, cos, sin)


ATTN_UNROLL = 4


def _attention_kernel(q_ref, k_ref, v_ref, o_ref, m_ref, acc_ref, *, tk, unroll):
    m_ref[...] = jnp.full(m_ref.shape, NEG_BIG, F32)
    acc_ref[...] = jnp.zeros(acc_ref.shape, F32)
    lane_tiles = tk // LANES

    def body(j, carry):
        r0 = pl.multiple_of(j * tk, tk)
        s = _dot_nt(q_ref[0], k_ref[0, pl.ds(r0, tk), :])
        tiles = [s[:, c * LANES:(c + 1) * LANES] for c in range(lane_tiles)]
        tile_max = functools.reduce(jnp.maximum, tiles)
        m_old = m_ref[...]
        m_new = jnp.maximum(m_old, jnp.max(tile_max, axis=1, keepdims=True))
        alpha = jnp.exp2(m_old - m_new)
        p = jnp.concatenate([jnp.exp2(t - m_new).astype(BF16) for t in tiles], axis=1)
        pv = _dot(p, v_ref[0, pl.ds(r0, tk), :])
        acc_ref[:, :V_HEAD] = alpha * acc_ref[:, :V_HEAD] + pv[:, :V_HEAD]
        acc_ref[:, V_HEAD:] = alpha * acc_ref[:, V_HEAD:] + pv[:, V_HEAD:]
        m_ref[...] = m_new
        return carry

    lax.fori_loop(0, k_ref.shape[1] // tk, body, 0, unroll=unroll)
    o_ref[0] = (acc_ref[:, :V_HEAD] / acc_ref[:, V_HEAD:]).astype(BF16)


def _attention(q, k, v, tq, tk):
    b, s, _ = q.shape
    return pl.pallas_call(
        functools.partial(_attention_kernel, tk=tk, unroll=min(ATTN_UNROLL, s // tk)),
        grid=(b, MLA_HEADS, s // tq),
        in_specs=[pl.BlockSpec((1, tq, HEAD_PAD), lambda i, h, j: (i, j, h)),
                  pl.BlockSpec((1, s, HEAD_PAD), lambda i, h, j: (i, 0, h)),
                  pl.BlockSpec((1, s, 2 * V_HEAD), lambda i, h, j: (i, 0, h))],
        out_specs=pl.BlockSpec((1, tq, V_HEAD), lambda i, h, j: (i, j, h)),
        out_shape=jax.ShapeDtypeStruct((b, s, MLA_WIDTH), BF16),
        scratch_shapes=[pltpu.VMEM((tq, LANES), F32), pltpu.VMEM((tq, 2 * V_HEAD), F32)],
        compiler_params=_params("arbitrary", "arbitrary", "arbitrary"),
        name="attention",
    )(q, k, v)


SSD_PREP_COLS = 512


def _ssd_prep_kernel(x_ref, ng_ref, sc_ref, sh_ref, w2_ref, dtb_ref, sz_ref, xbc_ref, dt_ref):
    hb = _modulated_norm(x_ref[0], ng_ref[...], sc_ref[0], sh_ref[0]).astype(BF16)
    for j in range(SSM_INNER // SSD_PREP_COLS):
        c0 = j * SSD_PREP_COLS
        sz_ref[0, :, c0:c0 + SSD_PREP_COLS] = _silu(_dot(hb, w2_ref[:, c0:c0 + SSD_PREP_COLS])).astype(BF16)
    for j in range(SSM_CONV_DIM // SSD_PREP_COLS):
        c0 = j * SSD_PREP_COLS
        w0 = SSM_INNER + c0
        xbc_ref[0, :, c0:c0 + SSD_PREP_COLS] = _dot(hb, w2_ref[:, w0:w0 + SSD_PREP_COLS]).astype(BF16)
    w0 = SSM_INNER + SSM_CONV_DIM
    dt_ref[0] = _softplus(_dot(hb, w2_ref[:, w0:w0 + LANES]) + dtb_ref[...])


def _ssd_prep(x, ng, scale, shift, w2, dtb, tm):
    b, s, _ = x.shape
    full = lambda a: pl.BlockSpec(a.shape, lambda i, j: (0,) * a.ndim)
    per_batch = pl.BlockSpec((1, 1, D_MODEL), lambda i, j: (i, 0, 0))
    row = lambda w: pl.BlockSpec((1, tm, w), lambda i, j: (i, j, 0))
    return pl.pallas_call(
        _ssd_prep_kernel,
        grid=(b, s // tm),
        in_specs=[row(D_MODEL), full(ng), per_batch, per_batch, full(w2), full(dtb)],
        out_specs=[row(SSM_INNER), row(SSM_CONV_DIM), row(LANES)],
        out_shape=[jax.ShapeDtypeStruct((b, s, SSM_INNER), BF16),
                   jax.ShapeDtypeStruct((b, s, SSM_CONV_DIM), BF16),
                   jax.ShapeDtypeStruct((b, s, LANES), F32)],
        compiler_params=_params("arbitrary", "arbitrary"),
        name="ssd_prep",
    )(x, ng, scale, shift, w2, dtb)


CONV_COLS = 512
CONV_ROW_BLOCK = 64

CONV_OFFSETS = tuple(d for d in range(-(CONV_WIDTH // 2), CONV_WIDTH // 2 + 1) if d != 0)


def _conv_columns(c0, prev_ref, cur_ref, next_ref, sh_ref, w_ref, b_ref, o_ref):
    j = pl.program_id(1)
    tc = cur_ref.shape[1]
    h = CONV_HALO
    mid = CONV_WIDTH // 2
    cols = slice(c0, c0 + CONV_COLS)
    taps = [w_ref[w:w + 1, cols] for w in range(CONV_WIDTH)]
    bias = b_ref[:, cols]
    cur = cur_ref[0, :, cols]
    prev = jnp.where(j > 0, prev_ref[0, :, cols].astype(F32), 0.0)
    nxt = jnp.where(j < pl.num_programs(1) - 1, next_ref[0, :, cols].astype(F32), 0.0)
    top_src = jnp.concatenate([prev, cur[0:h].astype(F32)], axis=0)
    bot_src = jnp.concatenate([cur[tc - h:tc].astype(F32), nxt], axis=0)
    t = lax.broadcasted_iota(jnp.int32, (h, 1), 0)
    top_fix = jnp.zeros((h, CONV_COLS), F32)
    bot_fix = jnp.zeros((h, CONV_COLS), F32)
    for d in CONV_OFFSETS:
        if d < 0:
            top_fix = top_fix + jnp.where(t + d < 0, pltpu.roll(top_src, -d, 0)[h:2 * h], 0.0) * taps[mid + d]
        else:
            bot_fix = bot_fix + jnp.where(t + d >= h, pltpu.roll(bot_src, 2 * h - d, 0)[0:h], 0.0) * taps[mid + d]
    nd = len(CONV_OFFSETS)
    rb = CONV_ROW_BLOCK
    for blk in range(tc // rb):
        r0 = blk * rb
        shifted = _dot(sh_ref[nd * r0:nd * (r0 + rb), :], cur)
        acc = cur[r0:r0 + rb].astype(F32) * taps[mid] + bias
        for i, d in enumerate(CONV_OFFSETS):
            acc = acc + shifted[i * rb:(i + 1) * rb] * taps[mid + d]
        lo, hi = 0, rb
        if blk == 0:
            o_ref[0, 0:h, cols] = _silu(acc[0:h] + top_fix).astype(BF16)
            lo = h
        if blk == tc // rb - 1:
            o_ref[0, tc - h:tc, cols] = _silu(acc[rb - h:rb] + bot_fix).astype(BF16)
            hi = rb - h
        o_ref[0, r0 + lo:r0 + hi, cols] = _silu(acc[lo:hi]).astype(BF16)


def _conv_kernel(*refs):
    for c0 in range(0, SSM_CONV_DIM, CONV_COLS):
        _conv_columns(c0, *refs)


def _conv(xbc, conv_w, conv_b, tc):
    b, s, _ = xbc.shape
    hb = tc // CONV_HALO
    last = s // CONV_HALO - 1
    src = jnp.arange(tc)[None, :]
    shifts = jnp.concatenate(
        [(src == jnp.arange(r0, r0 + CONV_ROW_BLOCK)[:, None] + d)
         for r0 in range(0, tc, CONV_ROW_BLOCK) for d in CONV_OFFSETS], axis=0).astype(BF16)
    full = lambda a: pl.BlockSpec(a.shape, lambda i, j: (0,) * a.ndim)
    return pl.pallas_call(
        _conv_kernel,
        grid=(b, s // tc),
        in_specs=[pl.BlockSpec((1, CONV_HALO, SSM_CONV_DIM), lambda i, j: (i, jnp.maximum(j * hb - 1, 0), 0)),
                  pl.BlockSpec((1, tc, SSM_CONV_DIM), lambda i, j: (i, j, 0)),
                  pl.BlockSpec((1, CONV_HALO, SSM_CONV_DIM), lambda i, j: (i, jnp.minimum((j + 1) * hb, last), 0)),
                  full(shifts), full(conv_w), full(conv_b)],
        out_specs=pl.BlockSpec((1, tc, SSM_CONV_DIM), lambda i, j: (i, j, 0)),
        out_shape=jax.ShapeDtypeStruct((b, s, SSM_CONV_DIM), BF16),
        compiler_params=_params("arbitrary", "arbitrary"),
        name="conv",
    )(xbc, xbc, xbc, shifts, conv_w, conv_b)


HEADS_PER_GROUP = SSM_HEADS // SSM_GROUPS


def _ssd_sweep(fwd, x_ref, b_ref, c_ref, dt_ref, sz_ref, par_ref, d_ref, nw_ref, o_ref, st_ref, yacc_ref,
               cps, nsteps):
    g = pl.program_id(1)
    s = pl.program_id(3)
    rows = cps * CHUNK
    sidx = s if fwd else nsteps - 1 - s
    shift = lax.rem(LANES - ((0 if fwd else SSM_HEADS) + g * HEADS_PER_GROUP), LANES)
    par = pltpu.roll(par_ref[...], shift, 1)
    a_coef = -jnp.exp(par[0:1, :]) * LOG2_E
    row = lax.broadcasted_iota(jnp.int32, (CHUNK, CHUNK), 0)
    col = lax.broadcasted_iota(jnp.int32, (CHUNK, CHUNK), 1)
    tri_mask = (row >= col) if fwd else (row <= col)
    tri = jnp.where(tri_mask, 1.0, 0.0).astype(BF16)
    low_half = col < SSM_HEADDIM
    lane = lax.broadcasted_iota(jnp.int32, (1, LANES), 1)
    lane_lo = jnp.where(lane < SSM_HEADDIM, 1.0, 0.0).astype(BF16)
    lane_hi = jnp.where(lane < SSM_HEADDIM, 0.0, 1.0).astype(BF16)
    erow = lax.broadcasted_iota(jnp.int32, (LANES, SSM_GROUP_WIDTH), 0)
    ecol = lax.broadcasted_iota(jnp.int32, (LANES, SSM_GROUP_WIDTH), 1)
    expand = jnp.where((ecol >= erow * SSM_HEADDIM) & (ecol < (erow + 1) * SSM_HEADDIM), 1.0, 0.0).astype(BF16)
    d_row = d_ref[0:1, :] if fwd else d_ref[1:2, :]
    far = CHUNK - 1 if fwd else 0

    for j in range(cps):
        r0 = (j if fwd else cps - 1 - j) * CHUNK
        xb = x_ref[0, r0:r0 + CHUNK, :]
        bm = b_ref[0, r0:r0 + CHUNK, :]
        cm = c_ref[0, r0:r0 + CHUNK, :]
        dt = pltpu.roll(dt_ref[0, r0:r0 + CHUNK, :], shift, 1)
        a = dt * a_coef
        acum = _split_dot(tri, a)
        acum_t = acum.T[0:HEADS_PER_GROUP, :]
        dt_t = dt.T[0:HEADS_PER_GROUP, :]
        alast = acum[far:far + 1, :]
        w_t = dt_t * jnp.exp2(acum_t[:, far:far + 1] - acum_t)
        src_t = acum_t - jnp.log2(dt_t)
        cdec = jnp.exp2(_split_dot(jnp.broadcast_to(alast, (8, LANES)), expand)[0:1, :])
        cb = _dot_nt(cm, bm)
        cm_f = cm.astype(F32)
        bm_t = bm.astype(F32).T
        st = st_ref[...]
        st_b = st.astype(BF16)
        y_parts = []
        for p in range(HEADS_PER_GROUP // 2):
            c0 = p * LANES
            xp = xb[:, c0:c0 + LANES]
            y_rhs = jnp.concatenate([xp, st_b[:, c0:c0 + LANES]], axis=0)

            def head_y(h):
                reach = jnp.broadcast_to(acum[:, h:h + 1], (CHUNK, CHUNK))
                mix = cb * jnp.exp2(jnp.where(tri_mask, reach - src_t[h:h + 1, :], NEG_BIG))
                carry_in = cm_f * jnp.exp2(reach)
                return _dot(jnp.concatenate([mix.astype(BF16), carry_in.astype(BF16)], axis=1), y_rhs)

            y_pair = jnp.where(low_half, head_y(2 * p), head_y(2 * p + 1))
            y_parts.append(y_pair + xp.astype(F32) * d_row[:, c0:c0 + LANES])
            s_lhs = jnp.concatenate([(bm_t * w_t[2 * p:2 * p + 1, :]).astype(BF16),
                                     (bm_t * w_t[2 * p + 1:2 * p + 2, :]).astype(BF16)], axis=1)
            s_rhs = jnp.concatenate([xp * lane_lo, xp * lane_hi], axis=0)
            st_ref[:, c0:c0 + LANES] = st[:, c0:c0 + LANES] * cdec[:, c0:c0 + LANES] + _dot(s_lhs, s_rhs)
        y = jnp.concatenate(y_parts, axis=1)
        grow = pl.multiple_of(sidx * rows + r0, CHUNK)
        if fwd:
            yacc_ref[pl.ds(grow, CHUNK), :] = y
        else:
            yt = (y + yacc_ref[pl.ds(grow, CHUNK), :]) * sz_ref[0, r0:r0 + CHUNK, :].astype(F32)
            o_ref[0, r0:r0 + CHUNK, :] = (_rms(yt) * nw_ref[...]).astype(BF16)


def _ssd_kernel(*refs, cps, nsteps):
    st_ref = refs[-2]
    ph = pl.program_id(2)

    @pl.when(pl.program_id(3) == 0)
    def _():
        st_ref[...] = jnp.zeros(st_ref.shape, F32)

    @pl.when(ph == 0)
    def _():
        _ssd_sweep(True, *refs, cps, nsteps)

    @pl.when(ph == 1)
    def _():
        _ssd_sweep(False, *refs, cps, nsteps)


def _ssd(xact, dt, sz, par, dexp, nw, cps):
    b, s, _ = xact.shape
    rows = cps * CHUNK
    nsteps = s // rows
    cur = lambda ph, t: jnp.where(ph == 0, t, nsteps - 1 - t)
    late = lambda ph, t: jnp.where(ph == 0, nsteps - 1, nsteps - 1 - t)
    gw = SSM_GROUP_WIDTH // LANES
    b_blk = SSM_INNER // SSM_STATE
    c_blk = b_blk + SSM_GROUPS
    return pl.pallas_call(
        functools.partial(_ssd_kernel, cps=cps, nsteps=nsteps),
        grid=(b, SSM_GROUPS, 2, nsteps),
        in_specs=[pl.BlockSpec((1, rows, SSM_GROUP_WIDTH), lambda i, g, ph, t: (i, cur(ph, t), g)),
                  pl.BlockSpec((1, rows, SSM_STATE), lambda i, g, ph, t: (i, cur(ph, t), b_blk + g)),
                  pl.BlockSpec((1, rows, SSM_STATE), lambda i, g, ph, t: (i, cur(ph, t), c_blk + g)),
                  pl.BlockSpec((1, rows, LANES), lambda i, g, ph, t: (i, cur(ph, t), 0)),
                  pl.BlockSpec((1, rows, SSM_GROUP_WIDTH), lambda i, g, ph, t: (i, late(ph, t), g)),
                  pl.BlockSpec((8, LANES), lambda i, g, ph, t: (0, 0)),
                  pl.BlockSpec((8, SSM_GROUP_WIDTH), lambda i, g, ph, t: (0, g)),
                  pl.BlockSpec((1, SSM_GROUP_WIDTH), lambda i, g, ph, t: (0, g))],
        out_specs=pl.BlockSpec((1, rows, SSM_GROUP_WIDTH), lambda i, g, ph, t: (i, late(ph, t), g)),
        out_shape=jax.ShapeDtypeStruct((b, s, SSM_INNER), BF16),
        scratch_shapes=[pltpu.VMEM((SSM_STATE, SSM_GROUP_WIDTH), F32),
                        pltpu.VMEM((s, SSM_GROUP_WIDTH), F32)],
        compiler_params=_params("arbitrary", "arbitrary", "arbitrary", "arbitrary"),
        name="ssd",
    )(xact, xact, xact, dt, sz, par, dexp, nw)


def _merge_kernel(x_ref, ng_ref, sc_ref, sh_ref, gt_ref, attn_ref, yn_ref, w3_ref, wpa_ref, wpb_ref, wo_ref,
                  o_ref):
    x = x_ref[0]
    hb = _modulated_norm(x, ng_ref[...], sc_ref[0], sh_ref[0]).astype(BF16)
    gate_a = _dot(hb, w3_ref[:, :MLA_WIDTH])
    a_in = (attn_ref[0].astype(F32) * _silu(gate_a)).astype(BF16)
    branch_a = _dot(a_in, wpa_ref[...])
    branch_b = _dot(yn_ref[0], wpb_ref[...])
    g_a = _dot(hb, w3_ref[:, MLA_WIDTH:MLA_WIDTH + D_MODEL])
    g_b = _dot(hb, w3_ref[:, MLA_WIDTH + D_MODEL:])
    merged = _sigmoid(g_a) * branch_a + _sigmoid(g_b) * branch_b
    out = _dot(merged.astype(BF16), wo_ref[...])
    o_ref[0] = x + gt_ref[0] * out


def _merge(x, ng, scale, shift, gate, attn, yn, w3, wpa, wpb, wo, tm):
    b, s, _ = x.shape
    full = lambda a: pl.BlockSpec(a.shape, lambda i, j: (0,) * a.ndim)
    per_batch = pl.BlockSpec((1, 1, D_MODEL), lambda i, j: (i, 0, 0))
    row = lambda w: pl.BlockSpec((1, tm, w), lambda i, j: (i, j, 0))
    return pl.pallas_call(
        _merge_kernel,
        grid=(b, s // tm),
        in_specs=[row(D_MODEL), full(ng), per_batch, per_batch, per_batch, row(MLA_WIDTH), row(SSM_INNER),
                  full(w3), full(wpa), full(wpb), full(wo)],
        out_specs=row(D_MODEL),
        out_shape=jax.ShapeDtypeStruct((b, s, D_MODEL), F32),
        compiler_params=_params("arbitrary", "arbitrary"),
        name="merge",
    )(x, ng, scale, shift, gate, attn, yn, w3, wpa, wpb, wo)


def _rope_tables(s):
    half = QK_ROPE // 2
    inv_freq = jnp.exp(-math.log(ROPE_THETA) * jnp.arange(half, dtype=F32) / half)
    ang = jnp.arange(s, dtype=F32)[:, None] * inv_freq[None, :]
    cos, sin = jnp.cos(ang), jnp.sin(ang)
    zeros = jnp.zeros((s, LANES - QK_ROPE), F32)
    return (jnp.concatenate([cos, cos, zeros], axis=1), jnp.concatenate([-sin, sin, zeros], axis=1))


def _tail_gain(g):
    return jnp.stack([g[:QK_NOPE], jnp.concatenate([g[QK_NOPE:], jnp.zeros((LANES - QK_ROPE,), F32)])])


def _prep_weights(norm_g, w_ada, b_ada, w_in, q_a_norm, w_q_up, kv_a_norm, w_kv_up, q_norm, k_norm, w_proj_a,
                  conv_w, conv_b, dt_bias_f, dt_bias_b, a_log_f, a_log_b, d_f, d_b, ssm_norm, w_proj_b, w_out):
    o_ga = Q_LORA + KV_LORA + QK_ROPE
    o_z = o_ga + MLA_WIDTH
    o_xbc = o_z + SSM_INNER
    o_dt = o_xbc + SSM_CONV_DIM
    o_gm = o_dt + 2 * SSM_HEADS
    zpad = jnp.zeros((D_MODEL, LANES - QK_ROPE), F32)
    w1 = jnp.concatenate([w_in[:, :o_ga], zpad], axis=1).astype(BF16)
    wq = jnp.pad(w_q_up.reshape(Q_LORA, MLA_HEADS, QK_HEAD), ((0, 0), (0, 0), (0, HEAD_PAD - QK_HEAD)))
    wq = wq.reshape(Q_LORA, MLA_HEADS * HEAD_PAD).astype(BF16)
    wkv = w_kv_up.reshape(KV_LORA, MLA_HEADS, 2, QK_NOPE).transpose(0, 2, 1, 3)
    wkv = wkv.reshape(KV_LORA, 2 * MLA_HEADS * QK_NOPE).astype(BF16)
    w2 = jnp.concatenate([w_in[:, o_z:o_gm], zpad], axis=1).astype(BF16)
    w3 = jnp.concatenate([w_in[:, o_ga:o_z], w_in[:, o_gm:]], axis=1).astype(BF16)
    lane_pad = jnp.zeros((LANES - 2 * SSM_HEADS,), F32)
    par = jnp.zeros((8, LANES), F32)
    par = par.at[0].set(jnp.concatenate([a_log_f, a_log_b, lane_pad]))
    dtb = jnp.concatenate([dt_bias_f, dt_bias_b, lane_pad]).reshape(1, LANES)
    dexp = jnp.zeros((8, SSM_INNER), F32)
    dexp = dexp.at[0].set(jnp.repeat(d_f, SSM_HEADDIM)).at[1].set(jnp.repeat(d_b, SSM_HEADDIM))
    return dict(
        ng=norm_g.reshape(1, D_MODEL), w_ada=w_ada.astype(BF16), b_ada=b_ada.reshape(1, 3 * D_MODEL),
        w1=w1, qan=q_a_norm.reshape(1, Q_LORA), wq=wq, kvan=kv_a_norm.reshape(1, KV_LORA), wkv=wkv,
        gq=_tail_gain(q_norm), gk=_tail_gain(k_norm), w2=w2, w3=w3,
        conv_w=jnp.pad(conv_w, ((0, 8 - CONV_WIDTH), (0, 0))), conv_b=conv_b.reshape(1, SSM_CONV_DIM),
        par=par, dtb=dtb, dexp=dexp, nw=ssm_norm.reshape(1, SSM_INNER),
        wpa=w_proj_a.astype(BF16), wpb=w_proj_b.astype(BF16), wo=w_out.astype(BF16))


def _tiles(s):
    return dict(tm=min(512, s), tq=min(1024, s), tk=min(512, s), tc=min(256, s), cps=min(4, s // CHUNK))


def _encoder_layer(x, mod, w, cos, sin):
    b, s, _ = x.shape
    t = _tiles(s)
    shift = mod[:, :D_MODEL].reshape(b, 1, D_MODEL)
    scale = mod[:, D_MODEL:2 * D_MODEL].reshape(b, 1, D_MODEL)
    gate = mod[:, 2 * D_MODEL:].reshape(b, 1, D_MODEL)
    q, k, v = _attn_prep(x, w["ng"], scale, shift, w["w1"], w["qan"], w["wq"], w["kvan"], w["wkv"],
                         w["gq"], w["gk"], cos, sin, t["tm"])
    attn = _attention(q, k, v, t["tq"], t["tk"])
    sz, xbc, dt = _ssd_prep(x, w["ng"], scale, shift, w["w2"], w["dtb"], t["tm"])
    xact = _conv(xbc, w["conv_w"], w["conv_b"], t["tc"])
    yn = _ssd(xact, dt, sz, w["par"], w["dexp"], w["nw"], t["cps"])
    return _merge(x, w["ng"], scale, shift, gate, attn, yn, w["w3"], w["wpa"], w["wpb"], w["wo"], t["tm"])


def kernel(x_prompt, x_sample, c_prompt, c_sample, norm_g, w_ada, b_ada, w_in, q_a_norm, w_q_up, kv_a_norm, w_kv_up, q_norm, k_norm, w_proj_a, conv_w, conv_b, dt_bias_f, dt_bias_b, a_log_f, a_log_b, d_f, d_b, ssm_norm, w_proj_b, w_out):
    layer_params = (norm_g, w_ada, b_ada, w_in, q_a_norm, w_q_up, kv_a_norm, w_kv_up, q_norm, k_norm, w_proj_a,
                    conv_w, conv_b, dt_bias_f, dt_bias_b, a_log_f, a_log_b, d_f, d_b, ssm_norm, w_proj_b, w_out)
    nb = x_prompt.shape[0]
    c_all = jnp.concatenate([c_prompt, c_sample], axis=0)
    c_all = jnp.pad(c_all, ((0, -c_all.shape[0] % 8), (0, 0)))
    tables = {s: _rope_tables(s) for s in {x_prompt.shape[1], x_sample.shape[1]}}
    y_prompt, y_sample = x_prompt, x_sample
    for l in range(norm_g.shape[0]):
        w = _prep_weights(*(p[l] for p in layer_params))
        mod = _mod(c_all, w["w_ada"], w["b_ada"])
        y_prompt = _encoder_layer(y_prompt, mod[:nb], w, *tables[y_prompt.shape[1]])
        y_sample = _encoder_layer(y_sample, mod[nb:nb + x_sample.shape[0]], w, *tables[y_sample.shape[1]])
    return (y_prompt, y_sample)
```

```python
import functools
import math

import jax
import jax.numpy as jnp
from jax import lax
from jax.experimental import pallas as pl
from jax.experimental.pallas import tpu as pltpu

D_MODEL = 1024
MLA_HEADS = 8
QK_NOPE = 128
QK_ROPE = 64
QK_HEAD = QK_NOPE + QK_ROPE
V_HEAD = 128
Q_LORA = 384
KV_LORA = 256
MLA_WIDTH = MLA_HEADS * V_HEAD
ROPE_THETA = 10000.0
HEAD_PAD = 256

SSM_INNER = 2048
SSM_HEADDIM = 64
SSM_HEADS = 32
SSM_GROUPS = 4
SSM_STATE = 128
SSM_GROUP_WIDTH = SSM_INNER // SSM_GROUPS
SSM_CONV_DIM = SSM_INNER + 2 * SSM_GROUPS * SSM_STATE
CONV_WIDTH = 5
CHUNK = 128
EPS = 1e-6

LANES = 128
CONV_HALO = 16
VMEM_LIMIT = 56 * 1024 * 1024

F32 = jnp.float32
BF16 = jnp.bfloat16
NEG_BIG = -1e30
LOG2_E = 1.4426950408889634


def _dot(a, b):
    return jnp.dot(a, b, preferred_element_type=F32)


def _dot_nt(a, b):
    return lax.dot_general(a, b, (((1,), (1,)), ((), ())), preferred_element_type=F32)


def _sigmoid(t):
    return 1.0 / (1.0 + jnp.exp2(t * (-LOG2_E)))


def _silu(t):
    return t * _sigmoid(t)


def _softplus(t):
    return jnp.maximum(t, 0.0) + jnp.log1p(jnp.exp(-jnp.abs(t)))


def _rms(t):
    return t * lax.rsqrt(jnp.mean(t * t, axis=-1, keepdims=True) + EPS)


def _modulated_norm(x, g, scale, shift):
    return (_rms(x) * g) * (1.0 + scale) + shift


def _split_dot(m, v):
    hi = v.astype(BF16)
    r1 = v - hi.astype(F32)
    mid = r1.astype(BF16)
    lo = (r1 - mid.astype(F32)).astype(BF16)
    return _dot(m, hi) + _dot(m, mid) + _dot(m, lo)


def _params(*sem):
    return pltpu.CompilerParams(dimension_semantics=sem, vmem_limit_bytes=VMEM_LIMIT)


def _mod_kernel(c_ref, w_ref, b_ref, o_ref):
    o_ref[...] = _dot(_silu(c_ref[...]).astype(BF16), w_ref[...]) + b_ref[...]


def _mod(c, w_ada, b_ada):
    n = c.shape[0]
    return pl.pallas_call(
        _mod_kernel,
        grid=(3,),
        in_specs=[pl.BlockSpec((n, D_MODEL), lambda j: (0, 0)),
                  pl.BlockSpec((D_MODEL, D_MODEL), lambda j: (0, j)),
                  pl.BlockSpec((1, D_MODEL), lambda j: (0, j))],
        out_specs=pl.BlockSpec((n, D_MODEL), lambda j: (0, j)),
        out_shape=jax.ShapeDtypeStruct((n, 3 * D_MODEL), F32),
        compiler_params=_params("arbitrary"),
        name="mod",
    )(c, w_ada, b_ada)


def _attn_prep_kernel(x_ref, ng_ref, sc_ref, sh_ref, w1_ref, qan_ref, wq_ref, kvan_ref, wkv_ref,
                      gq_ref, gk_ref, cos_ref, sin_ref, q_ref, k_ref, v_ref):
    for r0 in range(0, x_ref.shape[1], ATTN_PREP_ROWS):
        _attn_prep_rows(slice(r0, r0 + ATTN_PREP_ROWS), x_ref, ng_ref, sc_ref, sh_ref, w1_ref, qan_ref, wq_ref,
                        kvan_ref, wkv_ref, gq_ref, gk_ref, cos_ref, sin_ref, q_ref, k_ref, v_ref)


ATTN_PREP_ROWS = 128


def _attn_prep_rows(rows, x_ref, ng_ref, sc_ref, sh_ref, w1_ref, qan_ref, wq_ref, kvan_ref, wkv_ref,
                    gq_ref, gk_ref, tcs_ref, tsc_ref, q_ref, k_ref, v_ref):
    hb = _modulated_norm(x_ref[0, rows, :], ng_ref[...], sc_ref[0], sh_ref[0]).astype(BF16)
    p1 = _dot(hb, w1_ref[...])
    qn = (_rms(p1[:, :Q_LORA]) * qan_ref[...]).astype(BF16)
    kvn = (_rms(p1[:, Q_LORA:Q_LORA + KV_LORA]) * kvan_ref[...]).astype(BF16)
    kr = p1[:, Q_LORA + KV_LORA:Q_LORA + KV_LORA + LANES]
    kr_sw = p1[:, Q_LORA + KV_LORA + LANES:]
    tcs = tcs_ref[rows, :]
    tsc = tsc_ref[rows, :]
    gq_nope, gk_nope = gq_ref[0:1, :], gk_ref[0:1, :]
    q_tab = tcs * gq_ref[1:2, :]
    kr_rot = kr * (tcs * gk_ref[1:2, :]) + kr_sw * (tsc * gk_ref[2:3, :])
    kr_sq = (kr * kr) * 0.5
    ones = jnp.ones((hb.shape[0], V_HEAD), BF16)
    scale = QK_HEAD ** -0.5 * LOG2_E
    for h in range(MLA_HEADS):
        c0 = h * HEAD_PAD
        qa = _dot(qn, wq_ref[:, c0:c0 + HEAD_PAD])
        q_nope = qa[:, :QK_NOPE]
        q_tail = qa[:, QK_NOPE:]
        ss = jnp.sum(q_nope * q_nope + (q_tail * q_tail) * 0.5, axis=-1, keepdims=True)
        r = lax.rsqrt(ss * (1.0 / QK_HEAD) + EPS) * scale
        q_ref[0, rows, c0:c0 + QK_NOPE] = (q_nope * gq_nope * r).astype(BF16)
        q_ref[0, rows, c0 + QK_NOPE:c0 + HEAD_PAD] = (q_tail * q_tab * r).astype(BF16)
        kv = _dot(kvn, wkv_ref[:, c0:c0 + HEAD_PAD])
        k_nope = kv[:, :QK_NOPE]
        ssk = jnp.sum(k_nope * k_nope + kr_sq, axis=-1, keepdims=True)
        rk = lax.rsqrt(ssk * (1.0 / QK_HEAD) + EPS)
        k_ref[0, rows, c0:c0 + QK_NOPE] = (k_nope * gk_nope * rk).astype(BF16)
        k_ref[0, rows, c0 + QK_NOPE:c0 + HEAD_PAD] = (kr_rot * rk).astype(BF16)
        v_ref[0, rows, 2 * h * V_HEAD:(2 * h + 1) * V_HEAD] = kv[:, QK_NOPE:].astype(BF16)
        v_ref[0, rows, (2 * h + 1) * V_HEAD:(2 * h + 2) * V_HEAD] = ones


def _attn_prep(x, ng, scale, shift, w1, qan, wq, kvan, wkv, gq, gk, cos, sin, tm):
    b, s, _ = x.shape
    full = lambda a: pl.BlockSpec(a.shape, lambda i, j: (0,) * a.ndim)
    per_batch = pl.BlockSpec((1, 1, D_MODEL), lambda i, j: (i, 0, 0))
    return pl.pallas_call(
        _attn_prep_kernel,
        grid=(b, s // tm),
        in_specs=[pl.BlockSpec((1, tm, D_MODEL), lambda i, j: (i, j, 0)),
                  full(ng), per_batch, per_batch, full(w1), full(qan), full(wq), full(kvan), full(wkv),
                  full(gq), full(gk),
                  pl.BlockSpec((tm, LANES), lambda i, j: (j, 0)),
                  pl.BlockSpec((tm, LANES), lambda i, j: (j, 0))],
        out_specs=[pl.BlockSpec((1, tm, MLA_HEADS * HEAD_PAD), lambda i, j: (i, j, 0)),
                   pl.BlockSpec((1, tm, MLA_HEADS * HEAD_PAD), lambda i, j: (i, j, 0)),
                   pl.BlockSpec((1, tm, 2 * MLA_WIDTH), lambda i, j: (i, j, 0))],
        out_shape=[jax.ShapeDtypeStruct((b, s, MLA_HEADS * HEAD_PAD), BF16),
                   jax.ShapeDtypeStruct((b, s, MLA_HEADS * HEAD_PAD), BF16),
                   jax.ShapeDtypeStruct((b, s, 2 * MLA_WIDTH), BF16)],
        compiler_params=_params("arbitrary", "arbitrary"),
        name="attn_prep",
    )(x, ng, scale, shift, w1, qan, wq, kvan, wkv, gq, gk, cos, sin)


ATTN_UNROLL = 4


def _attention_kernel(q_ref, k_ref, v_ref, o_ref, m_ref, acc_ref, *, tk, unroll):
    m_ref[...] = jnp.full(m_ref.shape, NEG_BIG, F32)
    acc_ref[...] = jnp.zeros(acc_ref.shape, F32)
    lane_tiles = tk // LANES

    def body(j, carry):
        r0 = pl.multiple_of(j * tk, tk)
        s = _dot_nt(q_ref[0], k_ref[0, pl.ds(r0, tk), :])
        tiles = [s[:, c * LANES:(c + 1) * LANES] for c in range(lane_tiles)]
        tile_max = functools.reduce(jnp.maximum, tiles)
        m_old = m_ref[...]
        m_new = jnp.maximum(m_old, jnp.max(tile_max, axis=1, keepdims=True))
        alpha = jnp.exp2(m_old - m_new)
        p = jnp.concatenate([jnp.exp2(t - m_new).astype(BF16) for t in tiles], axis=1)
        pv = _dot(p, v_ref[0, pl.ds(r0, tk), :])
        acc_ref[:, :V_HEAD] = alpha * acc_ref[:, :V_HEAD] + pv[:, :V_HEAD]
        acc_ref[:, V_HEAD:] = alpha * acc_ref[:, V_HEAD:] + pv[:, V_HEAD:]
        m_ref[...] = m_new
        return carry

    lax.fori_loop(0, k_ref.shape[1] // tk, body, 0, unroll=unroll)
    o_ref[0] = (acc_ref[:, :V_HEAD] / acc_ref[:, V_HEAD:]).astype(BF16)


def _attention(q, k, v, tq, tk):
    b, s, _ = q.shape
    return pl.pallas_call(
        functools.partial(_attention_kernel, tk=tk, unroll=min(ATTN_UNROLL, s // tk)),
        grid=(b, MLA_HEADS, s // tq),
        in_specs=[pl.BlockSpec((1, tq, HEAD_PAD), lambda i, h, j: (i, j, h)),
                  pl.BlockSpec((1, s, HEAD_PAD), lambda i, h, j: (i, 0, h)),
                  pl.BlockSpec((1, s, 2 * V_HEAD), lambda i, h, j: (i, 0, h))],
        out_specs=pl.BlockSpec((1, tq, V_HEAD), lambda i, h, j: (i, j, h)),
        out_shape=jax.ShapeDtypeStruct((b, s, MLA_WIDTH), BF16),
        scratch_shapes=[pltpu.VMEM((tq, LANES), F32), pltpu.VMEM((tq, 2 * V_HEAD), F32)],
        compiler_params=_params("arbitrary", "arbitrary", "arbitrary"),
        name="attention",
    )(q, k, v)


SSD_PREP_COLS = 512


def _ssd_prep_kernel(x_ref, ng_ref, sc_ref, sh_ref, w2_ref, dtb_ref, sz_ref, xbc_ref, dt_ref):
    hb = _modulated_norm(x_ref[0], ng_ref[...], sc_ref[0], sh_ref[0]).astype(BF16)
    for j in range(SSM_INNER // SSD_PREP_COLS):
        c0 = j * SSD_PREP_COLS
        sz_ref[0, :, c0:c0 + SSD_PREP_COLS] = _silu(_dot(hb, w2_ref[:, c0:c0 + SSD_PREP_COLS])).astype(BF16)
    for j in range(SSM_CONV_DIM // SSD_PREP_COLS):
        c0 = j * SSD_PREP_COLS
        w0 = SSM_INNER + c0
        xbc_ref[0, :, c0:c0 + SSD_PREP_COLS] = _dot(hb, w2_ref[:, w0:w0 + SSD_PREP_COLS]).astype(BF16)
    w0 = SSM_INNER + SSM_CONV_DIM
    dt_ref[0] = _softplus(_dot(hb, w2_ref[:, w0:w0 + LANES]) + dtb_ref[...])


def _ssd_prep(x, ng, scale, shift, w2, dtb, tm):
    b, s, _ = x.shape
    full = lambda a: pl.BlockSpec(a.shape, lambda i, j: (0,) * a.ndim)
    per_batch = pl.BlockSpec((1, 1, D_MODEL), lambda i, j: (i, 0, 0))
    row = lambda w: pl.BlockSpec((1, tm, w), lambda i, j: (i, j, 0))
    return pl.pallas_call(
        _ssd_prep_kernel,
        grid=(b, s // tm),
        in_specs=[row(D_MODEL), full(ng), per_batch, per_batch, full(w2), full(dtb)],
        out_specs=[row(SSM_INNER), row(SSM_CONV_DIM), row(LANES)],
        out_shape=[jax.ShapeDtypeStruct((b, s, SSM_INNER), BF16),
                   jax.ShapeDtypeStruct((b, s, SSM_CONV_DIM), BF16),
                   jax.ShapeDtypeStruct((b, s, LANES), F32)],
        compiler_params=_params("arbitrary", "arbitrary"),
        name="ssd_prep",
    )(x, ng, scale, shift, w2, dtb)


CONV_COLS = 512
CONV_ROW_BLOCK = 64

CONV_OFFSETS = tuple(d for d in range(-(CONV_WIDTH // 2), CONV_WIDTH // 2 + 1) if d != 0)


def _conv_columns(c0, prev_ref, cur_ref, next_ref, sh_ref, w_ref, b_ref, o_ref):
    j = pl.program_id(1)
    tc = cur_ref.shape[1]
    h = CONV_HALO
    mid = CONV_WIDTH // 2
    cols = slice(c0, c0 + CONV_COLS)
    taps = [w_ref[w:w + 1, cols] for w in range(CONV_WIDTH)]
    bias = b_ref[:, cols]
    cur = cur_ref[0, :, cols]
    prev = jnp.where(j > 0, prev_ref[0, :, cols].astype(F32), 0.0)
    nxt = jnp.where(j < pl.num_programs(1) - 1, next_ref[0, :, cols].astype(F32), 0.0)
    top_src = jnp.concatenate([prev, cur[0:h].astype(F32)], axis=0)
    bot_src = jnp.concatenate([cur[tc - h:tc].astype(F32), nxt], axis=0)
    t = lax.broadcasted_iota(jnp.int32, (h, 1), 0)
    top_fix = jnp.zeros((h, CONV_COLS), F32)
    bot_fix = jnp.zeros((h, CONV_COLS), F32)
    for d in CONV_OFFSETS:
        if d < 0:
            top_fix = top_fix + jnp.where(t + d < 0, pltpu.roll(top_src, -d, 0)[h:2 * h], 0.0) * taps[mid + d]
        else:
            bot_fix = bot_fix + jnp.where(t + d >= h, pltpu.roll(bot_src, 2 * h - d, 0)[0:h], 0.0) * taps[mid + d]
    nd = len(CONV_OFFSETS)
    rb = CONV_ROW_BLOCK
    for blk in range(tc // rb):
        r0 = blk * rb
        shifted = _dot(sh_ref[nd * r0:nd * (r0 + rb), :], cur)
        acc = cur[r0:r0 + rb].astype(F32) * taps[mid] + bias
        for i, d in enumerate(CONV_OFFSETS):
            acc = acc + shifted[i * rb:(i + 1) * rb] * taps[mid + d]
        lo, hi = 0, rb
        if blk == 0:
            o_ref[0, 0:h, cols] = _silu(acc[0:h] + top_fix).astype(BF16)
            lo = h
        if blk == tc // rb - 1:
            o_ref[0, tc - h:tc, cols] = _silu(acc[rb - h:rb] + bot_fix).astype(BF16)
            hi = rb - h
        o_ref[0, r0 + lo:r0 + hi, cols] = _silu(acc[lo:hi]).astype(BF16)


def _conv_kernel(*refs):
    for c0 in range(0, SSM_CONV_DIM, CONV_COLS):
        _conv_columns(c0, *refs)


def _conv(xbc, conv_w, conv_b, tc):
    b, s, _ = xbc.shape
    hb = tc // CONV_HALO
    last = s // CONV_HALO - 1
    src = jnp.arange(tc)[None, :]
    shifts = jnp.concatenate(
        [(src == jnp.arange(r0, r0 + CONV_ROW_BLOCK)[:, None] + d)
         for r0 in range(0, tc, CONV_ROW_BLOCK) for d in CONV_OFFSETS], axis=0).astype(BF16)
    full = lambda a: pl.BlockSpec(a.shape, lambda i, j: (0,) * a.ndim)
    return pl.pallas_call(
        _conv_kernel,
        grid=(b, s // tc),
        in_specs=[pl.BlockSpec((1, CONV_HALO, SSM_CONV_DIM), lambda i, j: (i, jnp.maximum(j * hb - 1, 0), 0)),
                  pl.BlockSpec((1, tc, SSM_CONV_DIM), lambda i, j: (i, j, 0)),
                  pl.BlockSpec((1, CONV_HALO, SSM_CONV_DIM), lambda i, j: (i, jnp.minimum((j + 1) * hb, last), 0)),
                  full(shifts), full(conv_w), full(conv_b)],
        out_specs=pl.BlockSpec((1, tc, SSM_CONV_DIM), lambda i, j: (i, j, 0)),
        out_shape=jax.ShapeDtypeStruct((b, s, SSM_CONV_DIM), BF16),
        compiler_params=_params("arbitrary", "arbitrary"),
        name="conv",
    )(xbc, xbc, xbc, shifts, conv_w, conv_b)


HEADS_PER_GROUP = SSM_HEADS // SSM_GROUPS


def _ssd_sweep(fwd, x_ref, b_ref, c_ref, dt_ref, sz_ref, par_ref, d_ref, nw_ref, o_ref, st_ref, yacc_ref,
               cps, nsteps):
    g = pl.program_id(1)
    s = pl.program_id(3)
    rows = cps * CHUNK
    sidx = s if fwd else nsteps - 1 - s
    shift = lax.rem(LANES - ((0 if fwd else SSM_HEADS) + g * HEADS_PER_GROUP), LANES)
    par = pltpu.roll(par_ref[...], shift, 1)
    a_coef = -jnp.exp(par[0:1, :]) * LOG2_E
    row = lax.broadcasted_iota(jnp.int32, (CHUNK, CHUNK), 0)
    col = lax.broadcasted_iota(jnp.int32, (CHUNK, CHUNK), 1)
    tri_mask = (row >= col) if fwd else (row <= col)
    tri = jnp.where(tri_mask, 1.0, 0.0).astype(BF16)
    low_half = col < SSM_HEADDIM
    lane = lax.broadcasted_iota(jnp.int32, (1, LANES), 1)
    lane_lo = jnp.where(lane < SSM_HEADDIM, 1.0, 0.0).astype(BF16)
    lane_hi = jnp.where(lane < SSM_HEADDIM, 0.0, 1.0).astype(BF16)
    erow = lax.broadcasted_iota(jnp.int32, (LANES, SSM_GROUP_WIDTH), 0)
    ecol = lax.broadcasted_iota(jnp.int32, (LANES, SSM_GROUP_WIDTH), 1)
    expand = jnp.where((ecol >= erow * SSM_HEADDIM) & (ecol < (erow + 1) * SSM_HEADDIM), 1.0, 0.0).astype(BF16)
    d_row = d_ref[0:1, :] if fwd else d_ref[1:2, :]
    far = CHUNK - 1 if fwd else 0

    for j in range(cps):
        r0 = (j if fwd else cps - 1 - j) * CHUNK
        xb = x_ref[0, r0:r0 + CHUNK, :]
        bm = b_ref[0, r0:r0 + CHUNK, :]
        cm = c_ref[0, r0:r0 + CHUNK, :]
        dt = pltpu.roll(dt_ref[0, r0:r0 + CHUNK, :], shift, 1)
        a = dt * a_coef
        acum = _split_dot(tri, a)
        acum_t = acum.T[0:HEADS_PER_GROUP, :]
        dt_t = dt.T[0:HEADS_PER_GROUP, :]
        alast = acum[far:far + 1, :]
        w_t = dt_t * jnp.exp2(acum_t[:, far:far + 1] - acum_t)
        src_t = acum_t - jnp.log2(dt_t)
        cdec = jnp.exp2(_split_dot(jnp.broadcast_to(alast, (8, LANES)), expand)[0:1, :])
        cb = _dot_nt(cm, bm)
        cm_f = cm.astype(F32)
        bm_t = bm.astype(F32).T
        st = st_ref[...]
        st_b = st.astype(BF16)
        y_parts = []
        for p in range(HEADS_PER_GROUP // 2):
            c0 = p * LANES
            xp = xb[:, c0:c0 + LANES]
            y_rhs = jnp.concatenate([xp, st_b[:, c0:c0 + LANES]], axis=0)

            def head_y(h):
                reach = jnp.broadcast_to(acum[:, h:h + 1], (CHUNK, CHUNK))
                mix = cb * jnp.exp2(jnp.where(tri_mask, reach - src_t[h:h + 1, :], NEG_BIG))
                carry_in = cm_f * jnp.exp2(reach)
                return _dot(jnp.concatenate([mix.astype(BF16), carry_in.astype(BF16)], axis=1), y_rhs)

            y_pair = jnp.where(low_half, head_y(2 * p), head_y(2 * p + 1))
            y_parts.append(y_pair + xp.astype(F32) * d_row[:, c0:c0 + LANES])
            s_lhs = jnp.concatenate([(bm_t * w_t[2 * p:2 * p + 1, :]).astype(BF16),
                                     (bm_t * w_t[2 * p + 1:2 * p + 2, :]).astype(BF16)], axis=1)
            s_rhs = jnp.concatenate([xp * lane_lo, xp * lane_hi], axis=0)
            st_ref[:, c0:c0 + LANES] = st[:, c0:c0 + LANES] * cdec[:, c0:c0 + LANES] + _dot(s_lhs, s_rhs)
        y = jnp.concatenate(y_parts, axis=1)
        grow = pl.multiple_of(sidx * rows + r0, CHUNK)
        if fwd:
            yacc_ref[pl.ds(grow, CHUNK), :] = y
        else:
            yt = (y + yacc_ref[pl.ds(grow, CHUNK), :]) * sz_ref[0, r0:r0 + CHUNK, :].astype(F32)
            o_ref[0, r0:r0 + CHUNK, :] = (_rms(yt) * nw_ref[...]).astype(BF16)


def _ssd_kernel(*refs, cps, nsteps):
    st_ref = refs[-2]
    ph = pl.program_id(2)

    @pl.when(pl.program_id(3) == 0)
    def _():
        st_ref[...] = jnp.zeros(st_ref.shape, F32)

    @pl.when(ph == 0)
    def _():
        _ssd_sweep(True, *refs, cps, nsteps)

    @pl.when(ph == 1)
    def _():
        _ssd_sweep(False, *refs, cps, nsteps)


def _ssd(xact, dt, sz, par, dexp, nw, cps):
    b, s, _ = xact.shape
    rows = cps * CHUNK
    nsteps = s // rows
    cur = lambda ph, t: jnp.where(ph == 0, t, nsteps - 1 - t)
    late = lambda ph, t: jnp.where(ph == 0, nsteps - 1, nsteps - 1 - t)
    gw = SSM_GROUP_WIDTH // LANES
    b_blk = SSM_INNER // SSM_STATE
    c_blk = b_blk + SSM_GROUPS
    return pl.pallas_call(
        functools.partial(_ssd_kernel, cps=cps, nsteps=nsteps),
        grid=(b, SSM_GROUPS, 2, nsteps),
        in_specs=[pl.BlockSpec((1, rows, SSM_GROUP_WIDTH), lambda i, g, ph, t: (i, cur(ph, t), g)),
                  pl.BlockSpec((1, rows, SSM_STATE), lambda i, g, ph, t: (i, cur(ph, t), b_blk + g)),
                  pl.BlockSpec((1, rows, SSM_STATE), lambda i, g, ph, t: (i, cur(ph, t), c_blk + g)),
                  pl.BlockSpec((1, rows, LANES), lambda i, g, ph, t: (i, cur(ph, t), 0)),
                  pl.BlockSpec((1, rows, SSM_GROUP_WIDTH), lambda i, g, ph, t: (i, late(ph, t), g)),
                  pl.BlockSpec((8, LANES), lambda i, g, ph, t: (0, 0)),
                  pl.BlockSpec((8, SSM_GROUP_WIDTH), lambda i, g, ph, t: (0, g)),
                  pl.BlockSpec((1, SSM_GROUP_WIDTH), lambda i, g, ph, t: (0, g))],
        out_specs=pl.BlockSpec((1, rows, SSM_GROUP_WIDTH), lambda i, g, ph, t: (i, late(ph, t), g)),
        out_shape=jax.ShapeDtypeStruct((b, s, SSM_INNER), BF16),
        scratch_shapes=[pltpu.VMEM((SSM_STATE, SSM_GROUP_WIDTH), F32),
                        pltpu.VMEM((s, SSM_GROUP_WIDTH), F32)],
        compiler_params=_params("arbitrary", "arbitrary", "arbitrary", "arbitrary"),
        name="ssd",
    )(xact, xact, xact, dt, sz, par, dexp, nw)


def _merge_kernel(x_ref, ng_ref, sc_ref, sh_ref, gt_ref, attn_ref, yn_ref, w3_ref, wpa_ref, wpb_ref, wo_ref,
                  o_ref):
    x = x_ref[0]
    hb = _modulated_norm(x, ng_ref[...], sc_ref[0], sh_ref[0]).astype(BF16)
    gate_a = _dot(hb, w3_ref[:, :MLA_WIDTH])
    a_in = (attn_ref[0].astype(F32) * _silu(gate_a)).astype(BF16)
    branch_a = _dot(a_in, wpa_ref[...])
    branch_b = _dot(yn_ref[0], wpb_ref[...])
    g_a = _dot(hb, w3_ref[:, MLA_WIDTH:MLA_WIDTH + D_MODEL])
    g_b = _dot(hb, w3_ref[:, MLA_WIDTH + D_MODEL:])
    merged = _sigmoid(g_a) * branch_a + _sigmoid(g_b) * branch_b
    out = _dot(merged.astype(BF16), wo_ref[...])
    o_ref[0] = x + gt_ref[0] * out


def _merge(x, ng, scale, shift, gate, attn, yn, w3, wpa, wpb, wo, tm):
    b, s, _ = x.shape
    full = lambda a: pl.BlockSpec(a.shape, lambda i, j: (0,) * a.ndim)
    per_batch = pl.BlockSpec((1, 1, D_MODEL), lambda i, j: (i, 0, 0))
    row = lambda w: pl.BlockSpec((1, tm, w), lambda i, j: (i, j, 0))
    return pl.pallas_call(
        _merge_kernel,
        grid=(b, s // tm),
        in_specs=[row(D_MODEL), full(ng), per_batch, per_batch, per_batch, row(MLA_WIDTH), row(SSM_INNER),
                  full(w3), full(wpa), full(wpb), full(wo)],
        out_specs=row(D_MODEL),
        out_shape=jax.ShapeDtypeStruct((b, s, D_MODEL), F32),
        compiler_params=_params("arbitrary", "arbitrary"),
        name="merge",
    )(x, ng, scale, shift, gate, attn, yn, w3, wpa, wpb, wo)


def _rope_tables(s):
    half = QK_ROPE // 2
    inv_freq = jnp.exp(-math.log(ROPE_THETA) * jnp.arange(half, dtype=F32) / half)
    ang = jnp.arange(s, dtype=F32)[:, None] * inv_freq[None, :]
    cos, sin = jnp.cos(ang), jnp.sin(ang)
    c = jnp.concatenate([cos, cos], axis=1)
    sg = jnp.concatenate([-sin, sin], axis=1)
    return jnp.concatenate([c, sg], axis=1), jnp.concatenate([sg, c], axis=1)


def _swap_halves(t):
    half = QK_ROPE // 2
    return jnp.concatenate([t[..., half:], t[..., :half]], axis=-1)


def _tail_gain(g):
    gr = g[QK_NOPE:]
    return jnp.stack([g[:QK_NOPE], jnp.concatenate([gr, _swap_halves(gr)]), jnp.concatenate([_swap_halves(gr), gr])])


def _prep_weights(norm_g, w_ada, b_ada, w_in, q_a_norm, w_q_up, kv_a_norm, w_kv_up, q_norm, k_norm, w_proj_a,
                  conv_w, conv_b, dt_bias_f, dt_bias_b, a_log_f, a_log_b, d_f, d_b, ssm_norm, w_proj_b, w_out):
    o_ga = Q_LORA + KV_LORA + QK_ROPE
    o_z = o_ga + MLA_WIDTH
    o_xbc = o_z + SSM_INNER
    o_dt = o_xbc + SSM_CONV_DIM
    o_gm = o_dt + 2 * SSM_HEADS
    zpad = jnp.zeros((D_MODEL, LANES - QK_ROPE), F32)
    w_kr = w_in[:, o_ga - QK_ROPE:o_ga]
    w1 = jnp.concatenate([w_in[:, :o_ga], _swap_halves(w_kr), _swap_halves(w_kr), w_kr], axis=1).astype(BF16)
    wq = w_q_up.reshape(Q_LORA, MLA_HEADS, QK_HEAD)
    wq = jnp.concatenate([wq, _swap_halves(wq[..., QK_NOPE:])], axis=-1)
    wq = wq.reshape(Q_LORA, MLA_HEADS * HEAD_PAD).astype(BF16)
    wkv = w_kv_up.astype(BF16)
    w2 = jnp.concatenate([w_in[:, o_z:o_gm], zpad], axis=1).astype(BF16)
    w3 = jnp.concatenate([w_in[:, o_ga:o_z], w_in[:, o_gm:]], axis=1).astype(BF16)
    lane_pad = jnp.zeros((LANES - 2 * SSM_HEADS,), F32)
    par = jnp.zeros((8, LANES), F32)
    par = par.at[0].set(jnp.concatenate([a_log_f, a_log_b, lane_pad]))
    dtb = jnp.concatenate([dt_bias_f, dt_bias_b, lane_pad]).reshape(1, LANES)
    dexp = jnp.zeros((8, SSM_INNER), F32)
    dexp = dexp.at[0].set(jnp.repeat(d_f, SSM_HEADDIM)).at[1].set(jnp.repeat(d_b, SSM_HEADDIM))
    return dict(
        ng=norm_g.reshape(1, D_MODEL), w_ada=w_ada.astype(BF16), b_ada=b_ada.reshape(1, 3 * D_MODEL),
        w1=w1, qan=q_a_norm.reshape(1, Q_LORA), wq=wq, kvan=kv_a_norm.reshape(1, KV_LORA), wkv=wkv,
        gq=_tail_gain(q_norm), gk=_tail_gain(k_norm), w2=w2, w3=w3,
        conv_w=jnp.pad(conv_w, ((0, 8 - CONV_WIDTH), (0, 0))), conv_b=conv_b.reshape(1, SSM_CONV_DIM),
        par=par, dtb=dtb, dexp=dexp, nw=ssm_norm.reshape(1, SSM_INNER),
        wpa=w_proj_a.astype(BF16), wpb=w_proj_b.astype(BF16), wo=w_out.astype(BF16))


def _tiles(s):
    return dict(tm=min(512, s), tq=min(1024, s), tk=min(512, s), tc=min(256, s), cps=min(4, s // CHUNK))


def _encoder_layer(x, mod, w, cos, sin):
    b, s, _ = x.shape
    t = _tiles(s)
    shift = mod[:, :D_MODEL].reshape(b, 1, D_MODEL)
    scale = mod[:, D_MODEL:2 * D_MODEL].reshape(b, 1, D_MODEL)
    gate = mod[:, 2 * D_MODEL:].reshape(b, 1, D_MODEL)
    q, k, v = _attn_prep(x, w["ng"], scale, shift, w["w1"], w["qan"], w["wq"], w["kvan"], w["wkv"],
                         w["gq"], w["gk"], cos, sin, t["tm"])
    attn = _attention(q, k, v, t["tq"], t["tk"])
    sz, xbc, dt = _ssd_prep(x, w["ng"], scale, shift, w["w2"], w["dtb"], t["tm"])
    xact = _conv(xbc, w["conv_w"], w["conv_b"], t["tc"])
    yn = _ssd(xact, dt, sz, w["par"], w["dexp"], w["nw"], t["cps"])
    return _merge(x, w["ng"], scale, shift, gate, attn, yn, w["w3"], w["wpa"], w["wpb"], w["wo"], t["tm"])


def kernel(x_prompt, x_sample, c_prompt, c_sample, norm_g, w_ada, b_ada, w_in, q_a_norm, w_q_up, kv_a_norm, w_kv_up, q_norm, k_norm, w_proj_a, conv_w, conv_b, dt_bias_f, dt_bias_b, a_log_f, a_log_b, d_f, d_b, ssm_norm, w_proj_b, w_out):
    layer_params = (norm_g, w_ada, b_ada, w_in, q_a_norm, w_q_up, kv_a_norm, w_kv_up, q_norm, k_norm, w_proj_a,
                    conv_w, conv_b, dt_bias_f, dt_bias_b, a_log_f, a_log_b, d_f, d_b, ssm_norm, w_proj_b, w_out)
    nb = x_prompt.shape[0]
    c_all = jnp.concatenate([c_prompt, c_sample], axis=0)
    c_all = jnp.pad(c_all, ((0, -c_all.shape[0] % 8), (0, 0)))
    tables = {s: _rope_tables(s) for s in {x_prompt.shape[1], x_sample.shape[1]}}
    y_prompt, y_sample = x_prompt, x_sample
    for l in range(norm_g.shape[0]):
        w = _prep_weights(*(p[l] for p in layer_params))
        mod = _mod(c_all, w["w_ada"], w["b_ada"])
        y_prompt = _encoder_layer(y_prompt, mod[:nb], w, *tables[y_prompt.shape[1]])
        y_sample = _encoder_layer(y_sample, mod[nb:nb + x_sample.shape[0]], w, *tables[y_sample.shape[1]])
    return (y_prompt, y_sample)
```

```python
import functools
import math

import jax
import jax.numpy as jnp
from jax import lax
from jax.experimental import pallas as pl
from jax.experimental.pallas import tpu as pltpu

D_MODEL = 1024
MLA_HEADS = 8
QK_NOPE = 128
QK_ROPE = 64
QK_HEAD = QK_NOPE + QK_ROPE
V_HEAD = 128
Q_LORA = 384
KV_LORA = 256
MLA_WIDTH = MLA_HEADS * V_HEAD
ROPE_THETA = 10000.0
HEAD_PAD = 256

SSM_INNER = 2048
SSM_HEADDIM = 64
SSM_HEADS = 32
SSM_GROUPS = 4
SSM_STATE = 128
SSM_GROUP_WIDTH = SSM_INNER // SSM_GROUPS
SSM_CONV_DIM = SSM_INNER + 2 * SSM_GROUPS * SSM_STATE
CONV_WIDTH = 5
CHUNK = 128
EPS = 1e-6

LANES = 128
CONV_HALO = 16
VMEM_LIMIT = 56 * 1024 * 1024

F32 = jnp.float32
BF16 = jnp.bfloat16
NEG_BIG = -1e30
LOG2_E = 1.4426950408889634


def _dot(a, b):
    return jnp.dot(a, b, preferred_element_type=F32)


def _dot_nt(a, b):
    return lax.dot_general(a, b, (((1,), (1,)), ((), ())), preferred_element_type=F32)


def _sigmoid(t):
    return 1.0 / (1.0 + jnp.exp2(t * (-LOG2_E)))


def _silu(t):
    return t * _sigmoid(t)


def _softplus(t):
    return jnp.maximum(t, 0.0) + jnp.log1p(jnp.exp(-jnp.abs(t)))


def _rms(t):
    return t * lax.rsqrt(jnp.mean(t * t, axis=-1, keepdims=True) + EPS)


def _modulated_norm(x, g, scale, shift):
    return (_rms(x) * g) * (1.0 + scale) + shift


def _split_dot(m, v):
    hi = v.astype(BF16)
    r1 = v - hi.astype(F32)
    mid = r1.astype(BF16)
    lo = (r1 - mid.astype(F32)).astype(BF16)
    return _dot(m, hi) + _dot(m, mid) + _dot(m, lo)


def _params(*sem):
    return pltpu.CompilerParams(dimension_semantics=sem, vmem_limit_bytes=VMEM_LIMIT)


def _mod_kernel(c_ref, w_ref, b_ref, o_ref):
    o_ref[...] = _dot(_silu(c_ref[...]).astype(BF16), w_ref[...]) + b_ref[...]


def _mod(c, w_ada, b_ada):
    n = c.shape[0]
    return pl.pallas_call(
        _mod_kernel,
        grid=(3,),
        in_specs=[pl.BlockSpec((n, D_MODEL), lambda j: (0, 0)),
                  pl.BlockSpec((D_MODEL, D_MODEL), lambda j: (0, j)),
                  pl.BlockSpec((1, D_MODEL), lambda j: (0, j))],
        out_specs=pl.BlockSpec((n, D_MODEL), lambda j: (0, j)),
        out_shape=jax.ShapeDtypeStruct((n, 3 * D_MODEL), F32),
        compiler_params=_params("arbitrary"),
        name="mod",
    )(c, w_ada, b_ada)


def _attn_prep_kernel(x_ref, ng_ref, sc_ref, sh_ref, w1_ref, qan_ref, wq_ref, kvan_ref, wkv_ref,
                      gq_ref, gk_ref, cos_ref, sin_ref, q_ref, k_ref, v_ref):
    for r0 in range(0, x_ref.shape[1], ATTN_PREP_ROWS):
        _attn_prep_rows(slice(r0, r0 + ATTN_PREP_ROWS), x_ref, ng_ref, sc_ref, sh_ref, w1_ref, qan_ref, wq_ref,
                        kvan_ref, wkv_ref, gq_ref, gk_ref, cos_ref, sin_ref, q_ref, k_ref, v_ref)


ATTN_PREP_ROWS = 128


def _attn_prep_rows(rows, x_ref, ng_ref, sc_ref, sh_ref, w1_ref, qan_ref, wq_ref, kvan_ref, wkv_ref,
                    gq_ref, gk_ref, tcs_ref, tsc_ref, q_ref, k_ref, v_ref):
    hb = _modulated_norm(x_ref[0, rows, :], ng_ref[...], sc_ref[0], sh_ref[0]).astype(BF16)
    p1 = _dot(hb, w1_ref[...])
    qn = (_rms(p1[:, :Q_LORA]) * qan_ref[...]).astype(BF16)
    kvn = (_rms(p1[:, Q_LORA:Q_LORA + KV_LORA]) * kvan_ref[...]).astype(BF16)
    kr = p1[:, Q_LORA + KV_LORA:Q_LORA + KV_LORA + LANES]
    kr_sw = p1[:, Q_LORA + KV_LORA + LANES:]
    tcs = tcs_ref[rows, :]
    tsc = tsc_ref[rows, :]
    gq_nope, gk_nope = gq_ref[0:1, :], gk_ref[0:1, :]
    q_tab = tcs * gq_ref[1:2, :]
    kr_rot = kr * (tcs * gk_ref[1:2, :]) + kr_sw * (tsc * gk_ref[2:3, :])
    kr_sq = (kr * kr) * 0.5
    ones = jnp.ones((hb.shape[0], V_HEAD), BF16)
    scale = QK_HEAD ** -0.5 * LOG2_E
    for h in range(MLA_HEADS):
        c0 = h * HEAD_PAD
        qa = _dot(qn, wq_ref[:, c0:c0 + HEAD_PAD])
        q_nope = qa[:, :QK_NOPE]
        q_tail = qa[:, QK_NOPE:]
        ss = jnp.sum(q_nope * q_nope + (q_tail * q_tail) * 0.5, axis=-1, keepdims=True)
        r = lax.rsqrt(ss * (1.0 / QK_HEAD) + EPS) * scale
        q_ref[0, rows, c0:c0 + QK_NOPE] = (q_nope * gq_nope * r).astype(BF16)
        q_ref[0, rows, c0 + QK_NOPE:c0 + HEAD_PAD] = (q_tail * q_tab * r).astype(BF16)
        kv = _dot(kvn, wkv_ref[:, c0:c0 + HEAD_PAD])
        k_nope = kv[:, :QK_NOPE]
        ssk = jnp.sum(k_nope * k_nope + kr_sq, axis=-1, keepdims=True)
        rk = lax.rsqrt(ssk * (1.0 / QK_HEAD) + EPS)
        k_ref[0, rows, c0:c0 + QK_NOPE] = (k_nope * gk_nope * rk).astype(BF16)
        k_ref[0, rows, c0 + QK_NOPE:c0 + HEAD_PAD] = (kr_rot * rk).astype(BF16)
        v_ref[0, rows, 2 * h * V_HEAD:(2 * h + 1) * V_HEAD] = kv[:, QK_NOPE:].astype(BF16)
        v_ref[0, rows, (2 * h + 1) * V_HEAD:(2 * h + 2) * V_HEAD] = ones


def _attn_prep(x, ng, scale, shift, w1, qan, wq, kvan, wkv, gq, gk, cos, sin, tm):
    b, s, _ = x.shape
    full = lambda a: pl.BlockSpec(a.shape, lambda i, j: (0,) * a.ndim)
    per_batch = pl.BlockSpec((1, 1, D_MODEL), lambda i, j: (i, 0, 0))
    return pl.pallas_call(
        _attn_prep_kernel,
        grid=(b, s // tm),
        in_specs=[pl.BlockSpec((1, tm, D_MODEL), lambda i, j: (i, j, 0)),
                  full(ng), per_batch, per_batch, full(w1), full(qan), full(wq), full(kvan), full(wkv),
                  full(gq), full(gk),
                  pl.BlockSpec((tm, LANES), lambda i, j: (j, 0)),
                  pl.BlockSpec((tm, LANES), lambda i, j: (j, 0))],
        out_specs=[pl.BlockSpec((1, tm, MLA_HEADS * HEAD_PAD), lambda i, j: (i, j, 0)),
                   pl.BlockSpec((1, tm, MLA_HEADS * HEAD_PAD), lambda i, j: (i, j, 0)),
                   pl.BlockSpec((1, tm, 2 * MLA_WIDTH), lambda i, j: (i, j, 0))],
        out_shape=[jax.ShapeDtypeStruct((b, s, MLA_HEADS * HEAD_PAD), BF16),
                   jax.ShapeDtypeStruct((b, s, MLA_HEADS * HEAD_PAD), BF16),
                   jax.ShapeDtypeStruct((b, s, 2 * MLA_WIDTH), BF16)],
        compiler_params=_params("arbitrary", "arbitrary"),
        name="attn_prep",
    )(x, ng, scale, shift, w1, qan, wq, kvan, wkv, gq, gk, cos, sin)


ATTN_UNROLL = 8


def _attention_kernel(q_ref, k_ref, v_ref, o_ref, m_ref, acc_ref, *, tk, unroll):
    m_ref[...] = jnp.full(m_ref.shape, NEG_BIG, F32)
    acc_ref[...] = jnp.zeros(acc_ref.shape, F32)
    lane_tiles = tk // LANES

    def body(j, carry):
        r0 = pl.multiple_of(j * tk, tk)
        s = _dot_nt(q_ref[0], k_ref[0, pl.ds(r0, tk), :])
        tiles = [s[:, c * LANES:(c + 1) * LANES] for c in range(lane_tiles)]
        tile_max = functools.reduce(jnp.maximum, tiles)
        m_old = m_ref[...]
        m_new = jnp.maximum(m_old, jnp.max(tile_max, axis=1, keepdims=True))
        alpha = jnp.exp2(m_old - m_new)
        p = jnp.concatenate([jnp.exp2(t - m_new).astype(BF16) for t in tiles], axis=1)
        pv = _dot(p, v_ref[0, pl.ds(r0, tk), :])
        acc_ref[:, :V_HEAD] = alpha * acc_ref[:, :V_HEAD] + pv[:, :V_HEAD]
        acc_ref[:, V_HEAD:] = alpha * acc_ref[:, V_HEAD:] + pv[:, V_HEAD:]
        m_ref[...] = m_new
        return carry

    lax.fori_loop(0, k_ref.shape[1] // tk, body, 0, unroll=unroll)
    o_ref[0] = (acc_ref[:, :V_HEAD] / acc_ref[:, V_HEAD:]).astype(BF16)


def _attention(q, k, v, tq, tk):
    b, s, _ = q.shape
    return pl.pallas_call(
        functools.partial(_attention_kernel, tk=tk, unroll=min(ATTN_UNROLL, s // tk)),
        grid=(b, MLA_HEADS, s // tq),
        in_specs=[pl.BlockSpec((1, tq, HEAD_PAD), lambda i, h, j: (i, j, h)),
                  pl.BlockSpec((1, s, HEAD_PAD), lambda i, h, j: (i, 0, h)),
                  pl.BlockSpec((1, s, 2 * V_HEAD), lambda i, h, j: (i, 0, h))],
        out_specs=pl.BlockSpec((1, tq, V_HEAD), lambda i, h, j: (i, j, h)),
        out_shape=jax.ShapeDtypeStruct((b, s, MLA_WIDTH), BF16),
        scratch_shapes=[pltpu.VMEM((tq, LANES), F32), pltpu.VMEM((tq, 2 * V_HEAD), F32)],
        compiler_params=_params("arbitrary", "arbitrary", "arbitrary"),
        name="attention",
    )(q, k, v)


SSD_PREP_COLS = 512


def _ssd_prep_kernel(x_ref, ng_ref, sc_ref, sh_ref, w2_ref, dtb_ref, sz_ref, xbc_ref, dt_ref):
    hb = _modulated_norm(x_ref[0], ng_ref[...], sc_ref[0], sh_ref[0]).astype(BF16)
    for j in range(SSM_INNER // SSD_PREP_COLS):
        c0 = j * SSD_PREP_COLS
        sz_ref[0, :, c0:c0 + SSD_PREP_COLS] = _silu(_dot(hb, w2_ref[:, c0:c0 + SSD_PREP_COLS])).astype(BF16)
    for j in range(SSM_CONV_DIM // SSD_PREP_COLS):
        c0 = j * SSD_PREP_COLS
        w0 = SSM_INNER + c0
        xbc_ref[0, :, c0:c0 + SSD_PREP_COLS] = _dot(hb, w2_ref[:, w0:w0 + SSD_PREP_COLS]).astype(BF16)
    w0 = SSM_INNER + SSM_CONV_DIM
    dt_ref[0] = _softplus(_dot(hb, w2_ref[:, w0:w0 + LANES]) + dtb_ref[...])


def _ssd_prep(x, ng, scale, shift, w2, dtb, tm):
    b, s, _ = x.shape
    full = lambda a: pl.BlockSpec(a.shape, lambda i, j: (0,) * a.ndim)
    per_batch = pl.BlockSpec((1, 1, D_MODEL), lambda i, j: (i, 0, 0))
    row = lambda w: pl.BlockSpec((1, tm, w), lambda i, j: (i, j, 0))
    return pl.pallas_call(
        _ssd_prep_kernel,
        grid=(b, s // tm),
        in_specs=[row(D_MODEL), full(ng), per_batch, per_batch, full(w2), full(dtb)],
        out_specs=[row(SSM_INNER), row(SSM_CONV_DIM), row(LANES)],
        out_shape=[jax.ShapeDtypeStruct((b, s, SSM_INNER), BF16),
                   jax.ShapeDtypeStruct((b, s, SSM_CONV_DIM), BF16),
                   jax.ShapeDtypeStruct((b, s, LANES), F32)],
        compiler_params=_params("arbitrary", "arbitrary"),
        name="ssd_prep",
    )(x, ng, scale, shift, w2, dtb)


CONV_COLS = 512
CONV_ROW_BLOCK = 64

CONV_OFFSETS = tuple(d for d in range(-(CONV_WIDTH // 2), CONV_WIDTH // 2 + 1) if d != 0)


def _conv_columns(c0, prev_ref, cur_ref, next_ref, sh_ref, w_ref, b_ref, o_ref):
    j = pl.program_id(1)
    tc = cur_ref.shape[1]
    h = CONV_HALO
    mid = CONV_WIDTH // 2
    cols = slice(c0, c0 + CONV_COLS)
    taps = [w_ref[w:w + 1, cols] for w in range(CONV_WIDTH)]
    bias = b_ref[:, cols]
    cur = cur_ref[0, :, cols]
    prev = jnp.where(j > 0, prev_ref[0, :, cols].astype(F32), 0.0)
    nxt = jnp.where(j < pl.num_programs(1) - 1, next_ref[0, :, cols].astype(F32), 0.0)
    top_src = jnp.concatenate([prev, cur[0:h].astype(F32)], axis=0)
    bot_src = jnp.concatenate([cur[tc - h:tc].astype(F32), nxt], axis=0)
    t = lax.broadcasted_iota(jnp.int32, (h, 1), 0)
    top_fix = jnp.zeros((h, CONV_COLS), F32)
    bot_fix = jnp.zeros((h, CONV_COLS), F32)
    for d in CONV_OFFSETS:
        if d < 0:
            top_fix = top_fix + jnp.where(t + d < 0, pltpu.roll(top_src, -d, 0)[h:2 * h], 0.0) * taps[mid + d]
        else:
            bot_fix = bot_fix + jnp.where(t + d >= h, pltpu.roll(bot_src, 2 * h - d, 0)[0:h], 0.0) * taps[mid + d]
    nd = len(CONV_OFFSETS)
    rb = CONV_ROW_BLOCK
    for blk in range(tc // rb):
        r0 = blk * rb
        shifted = _dot(sh_ref[nd * r0:nd * (r0 + rb), :], cur)
        acc = cur[r0:r0 + rb].astype(F32) * taps[mid] + bias
        for i, d in enumerate(CONV_OFFSETS):
            acc = acc + shifted[i * rb:(i + 1) * rb] * taps[mid + d]
        lo, hi = 0, rb
        if blk == 0:
            o_ref[0, 0:h, cols] = _silu(acc[0:h] + top_fix).astype(BF16)
            lo = h
        if blk == tc // rb - 1:
            o_ref[0, tc - h:tc, cols] = _silu(acc[rb - h:rb] + bot_fix).astype(BF16)
            hi = rb - h
        o_ref[0, r0 + lo:r0 + hi, cols] = _silu(acc[lo:hi]).astype(BF16)


def _conv_kernel(*refs):
    for c0 in range(0, SSM_CONV_DIM, CONV_COLS):
        _conv_columns(c0, *refs)


def _conv(xbc, conv_w, conv_b, tc):
    b, s, _ = xbc.shape
    hb = tc // CONV_HALO
    last = s // CONV_HALO - 1
    src = jnp.arange(tc)[None, :]
    shifts = jnp.concatenate(
        [(src == jnp.arange(r0, r0 + CONV_ROW_BLOCK)[:, None] + d)
         for r0 in range(0, tc, CONV_ROW_BLOCK) for d in CONV_OFFSETS], axis=0).astype(BF16)
    full = lambda a: pl.BlockSpec(a.shape, lambda i, j: (0,) * a.ndim)
    return pl.pallas_call(
        _conv_kernel,
        grid=(b, s // tc),
        in_specs=[pl.BlockSpec((1, CONV_HALO, SSM_CONV_DIM), lambda i, j: (i, jnp.maximum(j * hb - 1, 0), 0)),
                  pl.BlockSpec((1, tc, SSM_CONV_DIM), lambda i, j: (i, j, 0)),
                  pl.BlockSpec((1, CONV_HALO, SSM_CONV_DIM), lambda i, j: (i, jnp.minimum((j + 1) * hb, last), 0)),
                  full(shifts), full(conv_w), full(conv_b)],
        out_specs=pl.BlockSpec((1, tc, SSM_CONV_DIM), lambda i, j: (i, j, 0)),
        out_shape=jax.ShapeDtypeStruct((b, s, SSM_CONV_DIM), BF16),
        compiler_params=_params("arbitrary", "arbitrary"),
        name="conv",
    )(xbc, xbc, xbc, shifts, conv_w, conv_b)


HEADS_PER_GROUP = SSM_HEADS // SSM_GROUPS
SSD_UNROLL = 4


def _ssd_sweep(fwd, x_ref, b_ref, c_ref, dt_ref, sz_ref, par_ref, d_ref, nw_ref, o_ref, st_ref, yacc_ref,
               cps, nsteps):
    g = pl.program_id(1)
    s = pl.program_id(3)
    rows = cps * CHUNK
    sidx = s if fwd else nsteps - 1 - s
    shift = lax.rem(LANES - ((0 if fwd else SSM_HEADS) + g * HEADS_PER_GROUP), LANES)
    par = pltpu.roll(par_ref[...], shift, 1)
    a_coef = -jnp.exp(par[0:1, :]) * LOG2_E
    row = lax.broadcasted_iota(jnp.int32, (CHUNK, CHUNK), 0)
    col = lax.broadcasted_iota(jnp.int32, (CHUNK, CHUNK), 1)
    tri_mask = (row >= col) if fwd else (row <= col)
    tri = jnp.where(tri_mask, 1.0, 0.0).astype(BF16)
    low_half = col < SSM_HEADDIM
    lane = lax.broadcasted_iota(jnp.int32, (1, LANES), 1)
    lane_lo = jnp.where(lane < SSM_HEADDIM, 1.0, 0.0).astype(BF16)
    lane_hi = jnp.where(lane < SSM_HEADDIM, 0.0, 1.0).astype(BF16)
    erow = lax.broadcasted_iota(jnp.int32, (LANES, SSM_GROUP_WIDTH), 0)
    ecol = lax.broadcasted_iota(jnp.int32, (LANES, SSM_GROUP_WIDTH), 1)
    expand = jnp.where((ecol >= erow * SSM_HEADDIM) & (ecol < (erow + 1) * SSM_HEADDIM), 1.0, 0.0).astype(BF16)
    d_row = d_ref[0:1, :] if fwd else d_ref[1:2, :]
    far = CHUNK - 1 if fwd else 0

    def chunk(r0):
        xb = x_ref[0, pl.ds(r0, CHUNK), :]
        bm = b_ref[0, pl.ds(r0, CHUNK), :]
        cm = c_ref[0, pl.ds(r0, CHUNK), :]
        dt = pltpu.roll(dt_ref[0, pl.ds(r0, CHUNK), :], shift, 1)
        a = dt * a_coef
        acum = _split_dot(tri, a)
        acum_t = acum.T[0:HEADS_PER_GROUP, :]
        dt_t = dt.T[0:HEADS_PER_GROUP, :]
        alast = acum[far:far + 1, :]
        w_t = dt_t * jnp.exp2(acum_t[:, far:far + 1] - acum_t)
        src_t = acum_t - jnp.log2(dt_t)
        cdec = jnp.exp2(_split_dot(jnp.broadcast_to(alast, (8, LANES)), expand)[0:1, :])
        cb = _dot_nt(cm, bm)
        cm_f = cm.astype(F32)
        bm_t = bm.astype(F32).T
        st = st_ref[...]
        st_b = st.astype(BF16)
        y_parts = []
        for p in range(HEADS_PER_GROUP // 2):
            c0 = p * LANES
            xp = xb[:, c0:c0 + LANES]
            y_rhs = jnp.concatenate([xp, st_b[:, c0:c0 + LANES]], axis=0)

            def head_y(h):
                reach = jnp.broadcast_to(acum[:, h:h + 1], (CHUNK, CHUNK))
                mix = cb * jnp.exp2(jnp.where(tri_mask, reach - src_t[h:h + 1, :], NEG_BIG))
                carry_in = cm_f * jnp.exp2(reach)
                return _dot(jnp.concatenate([mix.astype(BF16), carry_in.astype(BF16)], axis=1), y_rhs)

            y_pair = jnp.where(low_half, head_y(2 * p), head_y(2 * p + 1))
            y_parts.append(y_pair + xp.astype(F32) * d_row[:, c0:c0 + LANES])
            s_lhs = jnp.concatenate([(bm_t * w_t[2 * p:2 * p + 1, :]).astype(BF16),
                                     (bm_t * w_t[2 * p + 1:2 * p + 2, :]).astype(BF16)], axis=1)
            s_rhs = jnp.concatenate([xp * lane_lo, xp * lane_hi], axis=0)
            st_ref[:, c0:c0 + LANES] = st[:, c0:c0 + LANES] * cdec[:, c0:c0 + LANES] + _dot(s_lhs, s_rhs)
        y = jnp.concatenate(y_parts, axis=1)
        grow = pl.multiple_of(sidx * rows + r0, CHUNK)
        if fwd:
            yacc_ref[pl.ds(grow, CHUNK), :] = y
        else:
            yt = (y + yacc_ref[pl.ds(grow, CHUNK), :]) * sz_ref[0, pl.ds(r0, CHUNK), :].astype(F32)
            o_ref[0, pl.ds(r0, CHUNK), :] = (_rms(yt) * nw_ref[...]).astype(BF16)

    unroll = min(SSD_UNROLL, cps)
    groups = cps // unroll

    def group(gi, carry):
        base = (gi if fwd else groups - 1 - gi) * (unroll * CHUNK)
        for i in range(unroll):
            chunk(pl.multiple_of(base + (i if fwd else unroll - 1 - i) * CHUNK, CHUNK))
        return carry

    lax.fori_loop(0, groups, group, 0)


def _ssd_kernel(*refs, cps, nsteps):
    st_ref = refs[-2]
    ph = pl.program_id(2)

    @pl.when(pl.program_id(3) == 0)
    def _():
        st_ref[...] = jnp.zeros(st_ref.shape, F32)

    @pl.when(ph == 0)
    def _():
        _ssd_sweep(True, *refs, cps, nsteps)

    @pl.when(ph == 1)
    def _():
        _ssd_sweep(False, *refs, cps, nsteps)


def _ssd(xact, dt, sz, par, dexp, nw, cps):
    b, s, _ = xact.shape
    rows = cps * CHUNK
    nsteps = s // rows
    cur = lambda ph, t: jnp.where(ph == 0, t, nsteps - 1 - t)
    late = lambda ph, t: jnp.where(ph == 0, nsteps - 1, nsteps - 1 - t)
    gw = SSM_GROUP_WIDTH // LANES
    b_blk = SSM_INNER // SSM_STATE
    c_blk = b_blk + SSM_GROUPS
    return pl.pallas_call(
        functools.partial(_ssd_kernel, cps=cps, nsteps=nsteps),
        grid=(b, SSM_GROUPS, 2, nsteps),
        in_specs=[pl.BlockSpec((1, rows, SSM_GROUP_WIDTH), lambda i, g, ph, t: (i, cur(ph, t), g)),
                  pl.BlockSpec((1, rows, SSM_STATE), lambda i, g, ph, t: (i, cur(ph, t), b_blk + g)),
                  pl.BlockSpec((1, rows, SSM_STATE), lambda i, g, ph, t: (i, cur(ph, t), c_blk + g)),
                  pl.BlockSpec((1, rows, LANES), lambda i, g, ph, t: (i, cur(ph, t), 0)),
                  pl.BlockSpec((1, rows, SSM_GROUP_WIDTH), lambda i, g, ph, t: (i, late(ph, t), g)),
                  pl.BlockSpec((8, LANES), lambda i, g, ph, t: (0, 0)),
                  pl.BlockSpec((8, SSM_GROUP_WIDTH), lambda i, g, ph, t: (0, g)),
                  pl.BlockSpec((1, SSM_GROUP_WIDTH), lambda i, g, ph, t: (0, g))],
        out_specs=pl.BlockSpec((1, rows, SSM_GROUP_WIDTH), lambda i, g, ph, t: (i, late(ph, t), g)),
        out_shape=jax.ShapeDtypeStruct((b, s, SSM_INNER), BF16),
        scratch_shapes=[pltpu.VMEM((SSM_STATE, SSM_GROUP_WIDTH), F32),
                        pltpu.VMEM((s, SSM_GROUP_WIDTH), F32)],
        compiler_params=_params("arbitrary", "arbitrary", "arbitrary", "arbitrary"),
        name="ssd",
    )(xact, xact, xact, dt, sz, par, dexp, nw)


def _merge_kernel(x_ref, ng_ref, sc_ref, sh_ref, gt_ref, attn_ref, yn_ref, w3_ref, wpa_ref, wpb_ref, wo_ref,
                  o_ref):
    x = x_ref[0]
    hb = _modulated_norm(x, ng_ref[...], sc_ref[0], sh_ref[0]).astype(BF16)
    gate_a = _dot(hb, w3_ref[:, :MLA_WIDTH])
    a_in = (attn_ref[0].astype(F32) * _silu(gate_a)).astype(BF16)
    branch_a = _dot(a_in, wpa_ref[...])
    branch_b = _dot(yn_ref[0], wpb_ref[...])
    g_a = _dot(hb, w3_ref[:, MLA_WIDTH:MLA_WIDTH + D_MODEL])
    g_b = _dot(hb, w3_ref[:, MLA_WIDTH + D_MODEL:])
    merged = _sigmoid(g_a) * branch_a + _sigmoid(g_b) * branch_b
    out = _dot(merged.astype(BF16), wo_ref[...])
    o_ref[0] = x + gt_ref[0] * out


def _merge(x, ng, scale, shift, gate, attn, yn, w3, wpa, wpb, wo, tm):
    b, s, _ = x.shape
    full = lambda a: pl.BlockSpec(a.shape, lambda i, j: (0,) * a.ndim)
    per_batch = pl.BlockSpec((1, 1, D_MODEL), lambda i, j: (i, 0, 0))
    row = lambda w: pl.BlockSpec((1, tm, w), lambda i, j: (i, j, 0))
    return pl.pallas_call(
        _merge_kernel,
        grid=(b, s // tm),
        in_specs=[row(D_MODEL), full(ng), per_batch, per_batch, per_batch, row(MLA_WIDTH), row(SSM_INNER),
                  full(w3), full(wpa), full(wpb), full(wo)],
        out_specs=row(D_MODEL),
        out_shape=jax.ShapeDtypeStruct((b, s, D_MODEL), F32),
        compiler_params=_params("arbitrary", "arbitrary"),
        name="merge",
    )(x, ng, scale, shift, gate, attn, yn, w3, wpa, wpb, wo)


def _rope_tables(s):
    half = QK_ROPE // 2
    inv_freq = jnp.exp(-math.log(ROPE_THETA) * jnp.arange(half, dtype=F32) / half)
    ang = jnp.arange(s, dtype=F32)[:, None] * inv_freq[None, :]
    cos, sin = jnp.cos(ang), jnp.sin(ang)
    c = jnp.concatenate([cos, cos], axis=1)
    sg = jnp.concatenate([-sin, sin], axis=1)
    return jnp.concatenate([c, sg], axis=1), jnp.concatenate([sg, c], axis=1)


def _swap_halves(t):
    half = QK_ROPE // 2
    return jnp.concatenate([t[..., half:], t[..., :half]], axis=-1)


def _tail_gain(g):
    gr = g[QK_NOPE:]
    return jnp.stack([g[:QK_NOPE], jnp.concatenate([gr, _swap_halves(gr)]), jnp.concatenate([_swap_halves(gr), gr])])


def _prep_weights(norm_g, w_ada, b_ada, w_in, q_a_norm, w_q_up, kv_a_norm, w_kv_up, q_norm, k_norm, w_proj_a,
                  conv_w, conv_b, dt_bias_f, dt_bias_b, a_log_f, a_log_b, d_f, d_b, ssm_norm, w_proj_b, w_out):
    o_ga = Q_LORA + KV_LORA + QK_ROPE
    o_z = o_ga + MLA_WIDTH
    o_xbc = o_z + SSM_INNER
    o_dt = o_xbc + SSM_CONV_DIM
    o_gm = o_dt + 2 * SSM_HEADS
    zpad = jnp.zeros((D_MODEL, LANES - QK_ROPE), F32)
    w_kr = w_in[:, o_ga - QK_ROPE:o_ga]
    w1 = jnp.concatenate([w_in[:, :o_ga], _swap_halves(w_kr), _swap_halves(w_kr), w_kr], axis=1).astype(BF16)
    wq = w_q_up.reshape(Q_LORA, MLA_HEADS, QK_HEAD)
    wq = jnp.concatenate([wq, _swap_halves(wq[..., QK_NOPE:])], axis=-1)
    wq = wq.reshape(Q_LORA, MLA_HEADS * HEAD_PAD).astype(BF16)
    wkv = w_kv_up.astype(BF16)
    w2 = jnp.concatenate([w_in[:, o_z:o_gm], zpad], axis=1).astype(BF16)
    w3 = jnp.concatenate([w_in[:, o_ga:o_z], w_in[:, o_gm:]], axis=1).astype(BF16)
    lane_pad = jnp.zeros((LANES - 2 * SSM_HEADS,), F32)
    par = jnp.zeros((8, LANES), F32)
    par = par.at[0].set(jnp.concatenate([a_log_f, a_log_b, lane_pad]))
    dtb = jnp.concatenate([dt_bias_f, dt_bias_b, lane_pad]).reshape(1, LANES)
    dexp = jnp.zeros((8, SSM_INNER), F32)
    dexp = dexp.at[0].set(jnp.repeat(d_f, SSM_HEADDIM)).at[1].set(jnp.repeat(d_b, SSM_HEADDIM))
    return dict(
        ng=norm_g.reshape(1, D_MODEL), w_ada=w_ada.astype(BF16), b_ada=b_ada.reshape(1, 3 * D_MODEL),
        w1=w1, qan=q_a_norm.reshape(1, Q_LORA), wq=wq, kvan=kv_a_norm.reshape(1, KV_LORA), wkv=wkv,
        gq=_tail_gain(q_norm), gk=_tail_gain(k_norm), w2=w2, w3=w3,
        conv_w=jnp.pad(conv_w, ((0, 8 - CONV_WIDTH), (0, 0))), conv_b=conv_b.reshape(1, SSM_CONV_DIM),
        par=par, dtb=dtb, dexp=dexp, nw=ssm_norm.reshape(1, SSM_INNER),
        wpa=w_proj_a.astype(BF16), wpb=w_proj_b.astype(BF16), wo=w_out.astype(BF16))


def _tiles(s):
    return dict(tm=min(512, s), tq=min(1024, s), tk=min(512, s), tc=min(256, s), cps=min(8, s // CHUNK))


def _encoder_layer(x, mod, w, cos, sin):
    b, s, _ = x.shape
    t = _tiles(s)
    shift = mod[:, :D_MODEL].reshape(b, 1, D_MODEL)
    scale = mod[:, D_MODEL:2 * D_MODEL].reshape(b, 1, D_MODEL)
    gate = mod[:, 2 * D_MODEL:].reshape(b, 1, D_MODEL)
    q, k, v = _attn_prep(x, w["ng"], scale, shift, w["w1"], w["qan"], w["wq"], w["kvan"], w["wkv"],
                         w["gq"], w["gk"], cos, sin, t["tm"])
    attn = _attention(q, k, v, t["tq"], t["tk"])
    sz, xbc, dt = _ssd_prep(x, w["ng"], scale, shift, w["w2"], w["dtb"], t["tm"])
    xact = _conv(xbc, w["conv_w"], w["conv_b"], t["tc"])
    yn = _ssd(xact, dt, sz, w["par"], w["dexp"], w["nw"], t["cps"])
    return _merge(x, w["ng"], scale, shift, gate, attn, yn, w["w3"], w["wpa"], w["wpb"], w["wo"], t["tm"])


def kernel(x_prompt, x_sample, c_prompt, c_sample, norm_g, w_ada, b_ada, w_in, q_a_norm, w_q_up, kv_a_norm, w_kv_up, q_norm, k_norm, w_proj_a, conv_w, conv_b, dt_bias_f, dt_bias_b, a_log_f, a_log_b, d_f, d_b, ssm_norm, w_proj_b, w_out):
    layer_params = (norm_g, w_ada, b_ada, w_in, q_a_norm, w_q_up, kv_a_norm, w_kv_up, q_norm, k_norm, w_proj_a,
                    conv_w, conv_b, dt_bias_f, dt_bias_b, a_log_f, a_log_b, d_f, d_b, ssm_norm, w_proj_b, w_out)
    nb = x_prompt.shape[0]
    c_all = jnp.concatenate([c_prompt, c_sample], axis=0)
    c_all = jnp.pad(c_all, ((0, -c_all.shape[0] % 8), (0, 0)))
    tables = {s: _rope_tables(s) for s in {x_prompt.shape[1], x_sample.shape[1]}}
    y_prompt, y_sample = x_prompt, x_sample
    for l in range(norm_g.shape[0]):
        w = _prep_weights(*(p[l] for p in layer_params))
        mod = _mod(c_all, w["w_ada"], w["b_ada"])
        y_prompt = _encoder_layer(y_prompt, mod[:nb], w, *tables[y_prompt.shape[1]])
        y_sample = _encoder_layer(y_sample, mod[nb:nb + x_sample.shape[0]], w, *tables[y_sample.shape[1]])
    return (y_prompt, y_sample)
```

```python
import functools
import math

import jax
import jax.numpy as jnp
from jax import lax
from jax.experimental import pallas as pl
from jax.experimental.pallas import tpu as pltpu

D_MODEL = 1024
MLA_HEADS = 8
QK_NOPE = 128
QK_ROPE = 64
QK_HEAD = QK_NOPE + QK_ROPE
V_HEAD = 128
Q_LORA = 384
KV_LORA = 256
MLA_WIDTH = MLA_HEADS * V_HEAD
ROPE_THETA = 10000.0
HEAD_PAD = 256

SSM_INNER = 2048
SSM_HEADDIM = 64
SSM_HEADS = 32
SSM_GROUPS = 4
SSM_STATE = 128
SSM_GROUP_WIDTH = SSM_INNER // SSM_GROUPS
SSM_CONV_DIM = SSM_INNER + 2 * SSM_GROUPS * SSM_STATE
CONV_WIDTH = 5
CHUNK = 128
EPS = 1e-6

LANES = 128
CONV_HALO = 16
VMEM_LIMIT = 56 * 1024 * 1024

F32 = jnp.float32
BF16 = jnp.bfloat16
NEG_BIG = -1e30
LOG2_E = 1.4426950408889634


def _dot(a, b):
    return jnp.dot(a, b, preferred_element_type=F32)


def _dot_nt(a, b):
    return lax.dot_general(a, b, (((1,), (1,)), ((), ())), preferred_element_type=F32)


def _sigmoid(t):
    return 1.0 / (1.0 + jnp.exp2(t * (-LOG2_E)))


def _silu(t):
    return t * _sigmoid(t)


def _softplus(t):
    return jnp.maximum(t, 0.0) + jnp.log1p(jnp.exp(-jnp.abs(t)))


def _rms(t):
    return t * lax.rsqrt(jnp.mean(t * t, axis=-1, keepdims=True) + EPS)


def _modulated_norm(x, g, scale, shift):
    return (_rms(x) * g) * (1.0 + scale) + shift


def _split_dot(m, v):
    hi = v.astype(BF16)
    r1 = v - hi.astype(F32)
    mid = r1.astype(BF16)
    lo = (r1 - mid.astype(F32)).astype(BF16)
    return _dot(m, hi) + _dot(m, mid) + _dot(m, lo)


def _params(*sem):
    return pltpu.CompilerParams(dimension_semantics=sem, vmem_limit_bytes=VMEM_LIMIT)


def _mod_kernel(c_ref, w_ref, b_ref, o_ref):
    o_ref[...] = _dot(_silu(c_ref[...]).astype(BF16), w_ref[...]) + b_ref[...]


def _mod(c, w_ada, b_ada):
    n = c.shape[0]
    return pl.pallas_call(
        _mod_kernel,
        grid=(3,),
        in_specs=[pl.BlockSpec((n, D_MODEL), lambda j: (0, 0)),
                  pl.BlockSpec((D_MODEL, D_MODEL), lambda j: (0, j)),
                  pl.BlockSpec((1, D_MODEL), lambda j: (0, j))],
        out_specs=pl.BlockSpec((n, D_MODEL), lambda j: (0, j)),
        out_shape=jax.ShapeDtypeStruct((n, 3 * D_MODEL), F32),
        compiler_params=_params("arbitrary"),
        name="mod",
    )(c, w_ada, b_ada)


def _attn_prep_kernel(x_ref, ng_ref, sc_ref, sh_ref, w1_ref, qan_ref, wq_ref, kvan_ref, wkv_ref,
                      gq_ref, gk_ref, cos_ref, sin_ref, q_ref, k_ref, v_ref):
    for r0 in range(0, x_ref.shape[1], ATTN_PREP_ROWS):
        _attn_prep_rows(slice(r0, r0 + ATTN_PREP_ROWS), x_ref, ng_ref, sc_ref, sh_ref, w1_ref, qan_ref, wq_ref,
                        kvan_ref, wkv_ref, gq_ref, gk_ref, cos_ref, sin_ref, q_ref, k_ref, v_ref)


ATTN_PREP_ROWS = 128


def _attn_prep_rows(rows, x_ref, ng_ref, sc_ref, sh_ref, w1_ref, qan_ref, wq_ref, kvan_ref, wkv_ref,
                    gq_ref, gk_ref, tcs_ref, tsc_ref, q_ref, k_ref, v_ref):
    hb = _modulated_norm(x_ref[0, rows, :], ng_ref[...], sc_ref[0], sh_ref[0]).astype(BF16)
    p1 = _dot(hb, w1_ref[...])
    qn = (_rms(p1[:, :Q_LORA]) * qan_ref[...]).astype(BF16)
    kvn = (_rms(p1[:, Q_LORA:Q_LORA + KV_LORA]) * kvan_ref[...]).astype(BF16)
    kr = p1[:, Q_LORA + KV_LORA:Q_LORA + KV_LORA + LANES]
    kr_sw = p1[:, Q_LORA + KV_LORA + LANES:]
    tcs = tcs_ref[rows, :]
    tsc = tsc_ref[rows, :]
    gq_nope, gk_nope = gq_ref[0:1, :], gk_ref[0:1, :]
    q_tab = tcs * gq_ref[1:2, :]
    kr_rot = kr * (tcs * gk_ref[1:2, :]) + kr_sw * (tsc * gk_ref[2:3, :])
    kr_sq = (kr * kr) * 0.5
    ones = jnp.ones((hb.shape[0], V_HEAD), BF16)
    scale = QK_HEAD ** -0.5 * LOG2_E
    for h in range(MLA_HEADS):
        c0 = h * HEAD_PAD
        qa = _dot(qn, wq_ref[:, c0:c0 + HEAD_PAD])
        q_nope = qa[:, :QK_NOPE]
        q_tail = qa[:, QK_NOPE:]
        ss = jnp.sum(q_nope * q_nope + (q_tail * q_tail) * 0.5, axis=-1, keepdims=True)
        r = lax.rsqrt(ss * (1.0 / QK_HEAD) + EPS) * scale
        q_ref[0, rows, c0:c0 + QK_NOPE] = (q_nope * gq_nope * r).astype(BF16)
        q_ref[0, rows, c0 + QK_NOPE:c0 + HEAD_PAD] = (q_tail * q_tab * r).astype(BF16)
        kv = _dot(kvn, wkv_ref[:, c0:c0 + HEAD_PAD])
        k_nope = kv[:, :QK_NOPE]
        ssk = jnp.sum(k_nope * k_nope + kr_sq, axis=-1, keepdims=True)
        rk = lax.rsqrt(ssk * (1.0 / QK_HEAD) + EPS)
        k_ref[0, rows, c0:c0 + QK_NOPE] = (k_nope * gk_nope * rk).astype(BF16)
        k_ref[0, rows, c0 + QK_NOPE:c0 + HEAD_PAD] = (kr_rot * rk).astype(BF16)
        v_ref[0, rows, 2 * h * V_HEAD:(2 * h + 1) * V_HEAD] = kv[:, QK_NOPE:].astype(BF16)
        v_ref[0, rows, (2 * h + 1) * V_HEAD:(2 * h + 2) * V_HEAD] = ones


def _attn_prep(x, ng, scale, shift, w1, qan, wq, kvan, wkv, gq, gk, cos, sin, tm):
    b, s, _ = x.shape
    full = lambda a: pl.BlockSpec(a.shape, lambda i, j: (0,) * a.ndim)
    per_batch = pl.BlockSpec((1, 1, D_MODEL), lambda i, j: (i, 0, 0))
    return pl.pallas_call(
        _attn_prep_kernel,
        grid=(b, s // tm),
        in_specs=[pl.BlockSpec((1, tm, D_MODEL), lambda i, j: (i, j, 0)),
                  full(ng), per_batch, per_batch, full(w1), full(qan), full(wq), full(kvan), full(wkv),
                  full(gq), full(gk),
                  pl.BlockSpec((tm, LANES), lambda i, j: (j, 0)),
                  pl.BlockSpec((tm, LANES), lambda i, j: (j, 0))],
        out_specs=[pl.BlockSpec((1, tm, MLA_HEADS * HEAD_PAD), lambda i, j: (i, j, 0)),
                   pl.BlockSpec((1, tm, MLA_HEADS * HEAD_PAD), lambda i, j: (i, j, 0)),
                   pl.BlockSpec((1, tm, 2 * MLA_WIDTH), lambda i, j: (i, j, 0))],
        out_shape=[jax.ShapeDtypeStruct((b, s, MLA_HEADS * HEAD_PAD), BF16),
                   jax.ShapeDtypeStruct((b, s, MLA_HEADS * HEAD_PAD), BF16),
                   jax.ShapeDtypeStruct((b, s, 2 * MLA_WIDTH), BF16)],
        compiler_params=_params("arbitrary", "arbitrary"),
        name="attn_prep",
    )(x, ng, scale, shift, w1, qan, wq, kvan, wkv, gq, gk, cos, sin)


ATTN_UNROLL = 16


def _attention_kernel(q_ref, k_ref, v_ref, o_ref, m_ref, acc_ref, *, tk, unroll):
    m_ref[...] = jnp.full(m_ref.shape, NEG_BIG, F32)
    acc_ref[...] = jnp.zeros(acc_ref.shape, F32)
    lane_tiles = tk // LANES

    def body(j, carry):
        r0 = pl.multiple_of(j * tk, tk)
        s = _dot_nt(q_ref[0], k_ref[0, pl.ds(r0, tk), :])
        tiles = [s[:, c * LANES:(c + 1) * LANES] for c in range(lane_tiles)]
        tile_max = functools.reduce(jnp.maximum, tiles)
        m_old = m_ref[...]
        m_new = jnp.maximum(m_old, jnp.max(tile_max, axis=1, keepdims=True))
        alpha = jnp.exp2(m_old - m_new)
        p = jnp.concatenate([jnp.exp2(t - m_new).astype(BF16) for t in tiles], axis=1)
        pv = _dot(p, v_ref[0, pl.ds(r0, tk), :])
        acc_ref[:, :V_HEAD] = alpha * acc_ref[:, :V_HEAD] + pv[:, :V_HEAD]
        acc_ref[:, V_HEAD:] = alpha * acc_ref[:, V_HEAD:] + pv[:, V_HEAD:]
        m_ref[...] = m_new
        return carry

    lax.fori_loop(0, k_ref.shape[1] // tk, body, 0, unroll=unroll)
    o_ref[0] = (acc_ref[:, :V_HEAD] / acc_ref[:, V_HEAD:]).astype(BF16)


def _attention(q, k, v, tq, tk):
    b, s, _ = q.shape
    return pl.pallas_call(
        functools.partial(_attention_kernel, tk=tk, unroll=min(ATTN_UNROLL, s // tk)),
        grid=(b, MLA_HEADS, s // tq),
        in_specs=[pl.BlockSpec((1, tq, HEAD_PAD), lambda i, h, j: (i, j, h)),
                  pl.BlockSpec((1, s, HEAD_PAD), lambda i, h, j: (i, 0, h)),
                  pl.BlockSpec((1, s, 2 * V_HEAD), lambda i, h, j: (i, 0, h))],
        out_specs=pl.BlockSpec((1, tq, V_HEAD), lambda i, h, j: (i, j, h)),
        out_shape=jax.ShapeDtypeStruct((b, s, MLA_WIDTH), BF16),
        scratch_shapes=[pltpu.VMEM((tq, LANES), F32), pltpu.VMEM((tq, 2 * V_HEAD), F32)],
        compiler_params=_params("arbitrary", "arbitrary", "arbitrary"),
        name="attention",
    )(q, k, v)


SSD_PREP_COLS = 512


def _ssd_prep_kernel(x_ref, ng_ref, sc_ref, sh_ref, w2_ref, dtb_ref, sz_ref, xbc_ref, dt_ref):
    hb = _modulated_norm(x_ref[0], ng_ref[...], sc_ref[0], sh_ref[0]).astype(BF16)
    for j in range(SSM_INNER // SSD_PREP_COLS):
        c0 = j * SSD_PREP_COLS
        sz_ref[0, :, c0:c0 + SSD_PREP_COLS] = _silu(_dot(hb, w2_ref[:, c0:c0 + SSD_PREP_COLS])).astype(BF16)
    for j in range(SSM_CONV_DIM // SSD_PREP_COLS):
        c0 = j * SSD_PREP_COLS
        w0 = SSM_INNER + c0
        xbc_ref[0, :, c0:c0 + SSD_PREP_COLS] = _dot(hb, w2_ref[:, w0:w0 + SSD_PREP_COLS]).astype(BF16)
    w0 = SSM_INNER + SSM_CONV_DIM
    dt_ref[0] = _softplus(_dot(hb, w2_ref[:, w0:w0 + LANES]) + dtb_ref[...])


def _ssd_prep(x, ng, scale, shift, w2, dtb, tm):
    b, s, _ = x.shape
    full = lambda a: pl.BlockSpec(a.shape, lambda i, j: (0,) * a.ndim)
    per_batch = pl.BlockSpec((1, 1, D_MODEL), lambda i, j: (i, 0, 0))
    row = lambda w: pl.BlockSpec((1, tm, w), lambda i, j: (i, j, 0))
    return pl.pallas_call(
        _ssd_prep_kernel,
        grid=(b, s // tm),
        in_specs=[row(D_MODEL), full(ng), per_batch, per_batch, full(w2), full(dtb)],
        out_specs=[row(SSM_INNER), row(SSM_CONV_DIM), row(LANES)],
        out_shape=[jax.ShapeDtypeStruct((b, s, SSM_INNER), BF16),
                   jax.ShapeDtypeStruct((b, s, SSM_CONV_DIM), BF16),
                   jax.ShapeDtypeStruct((b, s, LANES), F32)],
        compiler_params=_params("arbitrary", "arbitrary"),
        name="ssd_prep",
    )(x, ng, scale, shift, w2, dtb)


CONV_COLS = 512
CONV_ROW_BLOCK = 64

CONV_OFFSETS = tuple(d for d in range(-(CONV_WIDTH // 2), CONV_WIDTH // 2 + 1) if d != 0)


def _conv_columns(c0, prev_ref, cur_ref, next_ref, sh_ref, w_ref, b_ref, o_ref):
    j = pl.program_id(1)
    tc = cur_ref.shape[1]
    h = CONV_HALO
    mid = CONV_WIDTH // 2
    cols = slice(c0, c0 + CONV_COLS)
    taps = [w_ref[w:w + 1, cols] for w in range(CONV_WIDTH)]
    bias = b_ref[:, cols]
    cur = cur_ref[0, :, cols]
    prev = jnp.where(j > 0, prev_ref[0, :, cols].astype(F32), 0.0)
    nxt = jnp.where(j < pl.num_programs(1) - 1, next_ref[0, :, cols].astype(F32), 0.0)
    top_src = jnp.concatenate([prev, cur[0:h].astype(F32)], axis=0)
    bot_src = jnp.concatenate([cur[tc - h:tc].astype(F32), nxt], axis=0)
    t = lax.broadcasted_iota(jnp.int32, (h, 1), 0)
    top_fix = jnp.zeros((h, CONV_COLS), F32)
    bot_fix = jnp.zeros((h, CONV_COLS), F32)
    for d in CONV_OFFSETS:
        if d < 0:
            top_fix = top_fix + jnp.where(t + d < 0, pltpu.roll(top_src, -d, 0)[h:2 * h], 0.0) * taps[mid + d]
        else:
            bot_fix = bot_fix + jnp.where(t + d >= h, pltpu.roll(bot_src, 2 * h - d, 0)[0:h], 0.0) * taps[mid + d]
    nd = len(CONV_OFFSETS)
    rb = CONV_ROW_BLOCK
    for blk in range(tc // rb):
        r0 = blk * rb
        shifted = _dot(sh_ref[nd * r0:nd * (r0 + rb), :], cur)
        acc = cur[r0:r0 + rb].astype(F32) * taps[mid] + bias
        for i, d in enumerate(CONV_OFFSETS):
            acc = acc + shifted[i * rb:(i + 1) * rb] * taps[mid + d]
        lo, hi = 0, rb
        if blk == 0:
            o_ref[0, 0:h, cols] = _silu(acc[0:h] + top_fix).astype(BF16)
            lo = h
        if blk == tc // rb - 1:
            o_ref[0, tc - h:tc, cols] = _silu(acc[rb - h:rb] + bot_fix).astype(BF16)
            hi = rb - h
        o_ref[0, r0 + lo:r0 + hi, cols] = _silu(acc[lo:hi]).astype(BF16)


def _conv_kernel(*refs):
    for c0 in range(0, SSM_CONV_DIM, CONV_COLS):
        _conv_columns(c0, *refs)


def _conv(xbc, conv_w, conv_b, tc):
    b, s, _ = xbc.shape
    hb = tc // CONV_HALO
    last = s // CONV_HALO - 1
    src = jnp.arange(tc)[None, :]
    shifts = jnp.concatenate(
        [(src == jnp.arange(r0, r0 + CONV_ROW_BLOCK)[:, None] + d)
         for r0 in range(0, tc, CONV_ROW_BLOCK) for d in CONV_OFFSETS], axis=0).astype(BF16)
    full = lambda a: pl.BlockSpec(a.shape, lambda i, j: (0,) * a.ndim)
    return pl.pallas_call(
        _conv_kernel,
        grid=(b, s // tc),
        in_specs=[pl.BlockSpec((1, CONV_HALO, SSM_CONV_DIM), lambda i, j: (i, jnp.maximum(j * hb - 1, 0), 0)),
                  pl.BlockSpec((1, tc, SSM_CONV_DIM), lambda i, j: (i, j, 0)),
                  pl.BlockSpec((1, CONV_HALO, SSM_CONV_DIM), lambda i, j: (i, jnp.minimum((j + 1) * hb, last), 0)),
                  full(shifts), full(conv_w), full(conv_b)],
        out_specs=pl.BlockSpec((1, tc, SSM_CONV_DIM), lambda i, j: (i, j, 0)),
        out_shape=jax.ShapeDtypeStruct((b, s, SSM_CONV_DIM), BF16),
        compiler_params=_params("arbitrary", "arbitrary"),
        name="conv",
    )(xbc, xbc, xbc, shifts, conv_w, conv_b)


HEADS_PER_GROUP = SSM_HEADS // SSM_GROUPS
SSD_UNROLL = 4


def _ssd_sweep(fwd, x_ref, b_ref, c_ref, dt_ref, sz_ref, par_ref, d_ref, nw_ref, o_ref, st_ref, yacc_ref,
               cps, nsteps):
    g = pl.program_id(1)
    s = pl.program_id(3)
    rows = cps * CHUNK
    sidx = s if fwd else nsteps - 1 - s
    shift = lax.rem(LANES - ((0 if fwd else SSM_HEADS) + g * HEADS_PER_GROUP), LANES)
    par = pltpu.roll(par_ref[...], shift, 1)
    a_coef = -jnp.exp(par[0:1, :]) * LOG2_E
    row = lax.broadcasted_iota(jnp.int32, (CHUNK, CHUNK), 0)
    col = lax.broadcasted_iota(jnp.int32, (CHUNK, CHUNK), 1)
    tri_mask = (row >= col) if fwd else (row <= col)
    tri = jnp.where(tri_mask, 1.0, 0.0).astype(BF16)
    low_half = col < SSM_HEADDIM
    lane = lax.broadcasted_iota(jnp.int32, (1, LANES), 1)
    lane_lo = jnp.where(lane < SSM_HEADDIM, 1.0, 0.0).astype(BF16)
    lane_hi = jnp.where(lane < SSM_HEADDIM, 0.0, 1.0).astype(BF16)
    erow = lax.broadcasted_iota(jnp.int32, (LANES, SSM_GROUP_WIDTH), 0)
    ecol = lax.broadcasted_iota(jnp.int32, (LANES, SSM_GROUP_WIDTH), 1)
    expand = jnp.where((ecol >= erow * SSM_HEADDIM) & (ecol < (erow + 1) * SSM_HEADDIM), 1.0, 0.0).astype(BF16)
    d_row = d_ref[0:1, :] if fwd else d_ref[1:2, :]
    far = CHUNK - 1 if fwd else 0

    def chunk(r0):
        xb = x_ref[0, pl.ds(r0, CHUNK), :]
        bm = b_ref[0, pl.ds(r0, CHUNK), :]
        cm = c_ref[0, pl.ds(r0, CHUNK), :]
        dt = pltpu.roll(dt_ref[0, pl.ds(r0, CHUNK), :], shift, 1)
        a = dt * a_coef
        acum = _split_dot(tri, a)
        acum_t = acum.T[0:HEADS_PER_GROUP, :]
        dt_t = dt.T[0:HEADS_PER_GROUP, :]
        alast = acum[far:far + 1, :]
        w_t = dt_t * jnp.exp2(acum_t[:, far:far + 1] - acum_t)
        src_t = acum_t - jnp.log2(dt_t)
        cdec = jnp.exp2(_split_dot(jnp.broadcast_to(alast, (8, LANES)), expand)[0:1, :])
        cb = _dot_nt(cm, bm)
        cm_f = cm.astype(F32)
        bm_t = bm.astype(F32).T
        st = st_ref[...]
        st_b = st.astype(BF16)
        y_parts = []
        for p in range(HEADS_PER_GROUP // 2):
            c0 = p * LANES
            xp = xb[:, c0:c0 + LANES]
            y_rhs = jnp.concatenate([xp, st_b[:, c0:c0 + LANES]], axis=0)

            def head_y(h):
                reach = jnp.broadcast_to(acum[:, h:h + 1], (CHUNK, CHUNK))
                mix = cb * jnp.exp2(jnp.where(tri_mask, reach - src_t[h:h + 1, :], NEG_BIG))
                carry_in = cm_f * jnp.exp2(reach)
                return _dot(jnp.concatenate([mix.astype(BF16), carry_in.astype(BF16)], axis=1), y_rhs)

            y_pair = jnp.where(low_half, head_y(2 * p), head_y(2 * p + 1))
            y_parts.append(y_pair + xp.astype(F32) * d_row[:, c0:c0 + LANES])
            s_lhs = jnp.concatenate([(bm_t * w_t[2 * p:2 * p + 1, :]).astype(BF16),
                                     (bm_t * w_t[2 * p + 1:2 * p + 2, :]).astype(BF16)], axis=1)
            s_rhs = jnp.concatenate([xp * lane_lo, xp * lane_hi], axis=0)
            st_ref[:, c0:c0 + LANES] = st[:, c0:c0 + LANES] * cdec[:, c0:c0 + LANES] + _dot(s_lhs, s_rhs)
        y = jnp.concatenate(y_parts, axis=1)
        grow = pl.multiple_of(sidx * rows + r0, CHUNK)
        if fwd:
            yacc_ref[pl.ds(grow, CHUNK), :] = y
        else:
            yt = (y + yacc_ref[pl.ds(grow, CHUNK), :]) * sz_ref[0, pl.ds(r0, CHUNK), :].astype(F32)
            o_ref[0, pl.ds(r0, CHUNK), :] = (_rms(yt) * nw_ref[...]).astype(BF16)

    unroll = min(SSD_UNROLL, cps)
    groups = cps // unroll

    def group(gi, carry):
        base = (gi if fwd else groups - 1 - gi) * (unroll * CHUNK)
        for i in range(unroll):
            chunk(pl.multiple_of(base + (i if fwd else unroll - 1 - i) * CHUNK, CHUNK))
        return carry

    lax.fori_loop(0, groups, group, 0)


def _ssd_kernel(*refs, cps, nsteps):
    st_ref = refs[-2]
    ph = pl.program_id(2)

    @pl.when(pl.program_id(3) == 0)
    def _():
        st_ref[...] = jnp.zeros(st_ref.shape, F32)

    @pl.when(ph == 0)
    def _():
        _ssd_sweep(True, *refs, cps, nsteps)

    @pl.when(ph == 1)
    def _():
        _ssd_sweep(False, *refs, cps, nsteps)


def _ssd(xact, dt, sz, par, dexp, nw, cps):
    b, s, _ = xact.shape
    rows = cps * CHUNK
    nsteps = s // rows
    cur = lambda ph, t: jnp.where(ph == 0, t, nsteps - 1 - t)
    late = lambda ph, t: jnp.where(ph == 0, nsteps - 1, nsteps - 1 - t)
    gw = SSM_GROUP_WIDTH // LANES
    b_blk = SSM_INNER // SSM_STATE
    c_blk = b_blk + SSM_GROUPS
    return pl.pallas_call(
        functools.partial(_ssd_kernel, cps=cps, nsteps=nsteps),
        grid=(b, SSM_GROUPS, 2, nsteps),
        in_specs=[pl.BlockSpec((1, rows, SSM_GROUP_WIDTH), lambda i, g, ph, t: (i, cur(ph, t), g)),
                  pl.BlockSpec((1, rows, SSM_STATE), lambda i, g, ph, t: (i, cur(ph, t), b_blk + g)),
                  pl.BlockSpec((1, rows, SSM_STATE), lambda i, g, ph, t: (i, cur(ph, t), c_blk + g)),
                  pl.BlockSpec((1, rows, LANES), lambda i, g, ph, t: (i, cur(ph, t), 0)),
                  pl.BlockSpec((1, rows, SSM_GROUP_WIDTH), lambda i, g, ph, t: (i, late(ph, t), g)),
                  pl.BlockSpec((8, LANES), lambda i, g, ph, t: (0, 0)),
                  pl.BlockSpec((8, SSM_GROUP_WIDTH), lambda i, g, ph, t: (0, g)),
                  pl.BlockSpec((1, SSM_GROUP_WIDTH), lambda i, g, ph, t: (0, g))],
        out_specs=pl.BlockSpec((1, rows, SSM_GROUP_WIDTH), lambda i, g, ph, t: (i, late(ph, t), g)),
        out_shape=jax.ShapeDtypeStruct((b, s, SSM_INNER), BF16),
        scratch_shapes=[pltpu.VMEM((SSM_STATE, SSM_GROUP_WIDTH), F32),
                        pltpu.VMEM((s, SSM_GROUP_WIDTH), F32)],
        compiler_params=_params("arbitrary", "arbitrary", "arbitrary", "arbitrary"),
        name="ssd",
    )(xact, xact, xact, dt, sz, par, dexp, nw)


def _merge_kernel(x_ref, ng_ref, sc_ref, sh_ref, gt_ref, attn_ref, yn_ref, w3_ref, wpa_ref, wpb_ref, wo_ref,
                  o_ref):
    x = x_ref[0]
    hb = _modulated_norm(x, ng_ref[...], sc_ref[0], sh_ref[0]).astype(BF16)
    gate_a = _dot(hb, w3_ref[:, :MLA_WIDTH])
    a_in = (attn_ref[0].astype(F32) * _silu(gate_a)).astype(BF16)
    branch_a = _dot(a_in, wpa_ref[...])
    branch_b = _dot(yn_ref[0], wpb_ref[...])
    g_a = _dot(hb, w3_ref[:, MLA_WIDTH:MLA_WIDTH + D_MODEL])
    g_b = _dot(hb, w3_ref[:, MLA_WIDTH + D_MODEL:])
    merged = _sigmoid(g_a) * branch_a + _sigmoid(g_b) * branch_b
    out = _dot(merged.astype(BF16), wo_ref[...])
    o_ref[0] = x + gt_ref[0] * out


def _merge(x, ng, scale, shift, gate, attn, yn, w3, wpa, wpb, wo, tm):
    b, s, _ = x.shape
    full = lambda a: pl.BlockSpec(a.shape, lambda i, j: (0,) * a.ndim)
    per_batch = pl.BlockSpec((1, 1, D_MODEL), lambda i, j: (i, 0, 0))
    row = lambda w: pl.BlockSpec((1, tm, w), lambda i, j: (i, j, 0))
    return pl.pallas_call(
        _merge_kernel,
        grid=(b, s // tm),
        in_specs=[row(D_MODEL), full(ng), per_batch, per_batch, per_batch, row(MLA_WIDTH), row(SSM_INNER),
                  full(w3), full(wpa), full(wpb), full(wo)],
        out_specs=row(D_MODEL),
        out_shape=jax.ShapeDtypeStruct((b, s, D_MODEL), F32),
        compiler_params=_params("arbitrary", "arbitrary"),
        name="merge",
    )(x, ng, scale, shift, gate, attn, yn, w3, wpa, wpb, wo)


def _rope_tables(s):
    half = QK_ROPE // 2
    inv_freq = jnp.exp(-math.log(ROPE_THETA) * jnp.arange(half, dtype=F32) / half)
    ang = jnp.arange(s, dtype=F32)[:, None] * inv_freq[None, :]
    cos, sin = jnp.cos(ang), jnp.sin(ang)
    c = jnp.concatenate([cos, cos], axis=1)
    sg = jnp.concatenate([-sin, sin], axis=1)
    return jnp.concatenate([c, sg], axis=1), jnp.concatenate([sg, c], axis=1)


def _swap_halves(t):
    half = QK_ROPE // 2
    return jnp.concatenate([t[..., half:], t[..., :half]], axis=-1)


def _tail_gain(g):
    gr = g[QK_NOPE:]
    return jnp.stack([g[:QK_NOPE], jnp.concatenate([gr, _swap_halves(gr)]), jnp.concatenate([_swap_halves(gr), gr])])


def _prep_weights(norm_g, w_ada, b_ada, w_in, q_a_norm, w_q_up, kv_a_norm, w_kv_up, q_norm, k_norm, w_proj_a,
                  conv_w, conv_b, dt_bias_f, dt_bias_b, a_log_f, a_log_b, d_f, d_b, ssm_norm, w_proj_b, w_out):
    o_ga = Q_LORA + KV_LORA + QK_ROPE
    o_z = o_ga + MLA_WIDTH
    o_xbc = o_z + SSM_INNER
    o_dt = o_xbc + SSM_CONV_DIM
    o_gm = o_dt + 2 * SSM_HEADS
    zpad = jnp.zeros((D_MODEL, LANES - QK_ROPE), F32)
    w_kr = w_in[:, o_ga - QK_ROPE:o_ga]
    w1 = jnp.concatenate([w_in[:, :o_ga], _swap_halves(w_kr), _swap_halves(w_kr), w_kr], axis=1).astype(BF16)
    wq = w_q_up.reshape(Q_LORA, MLA_HEADS, QK_HEAD)
    wq = jnp.concatenate([wq, _swap_halves(wq[..., QK_NOPE:])], axis=-1)
    wq = wq.reshape(Q_LORA, MLA_HEADS * HEAD_PAD).astype(BF16)
    wkv = w_kv_up.astype(BF16)
    w2 = jnp.concatenate([w_in[:, o_z:o_gm], zpad], axis=1).astype(BF16)
    w3 = jnp.concatenate([w_in[:, o_ga:o_z], w_in[:, o_gm:]], axis=1).astype(BF16)
    lane_pad = jnp.zeros((LANES - 2 * SSM_HEADS,), F32)
    par = jnp.zeros((8, LANES), F32)
    par = par.at[0].set(jnp.concatenate([a_log_f, a_log_b, lane_pad]))
    dtb = jnp.concatenate([dt_bias_f, dt_bias_b, lane_pad]).reshape(1, LANES)
    dexp = jnp.zeros((8, SSM_INNER), F32)
    dexp = dexp.at[0].set(jnp.repeat(d_f, SSM_HEADDIM)).at[1].set(jnp.repeat(d_b, SSM_HEADDIM))
    return dict(
        ng=norm_g.reshape(1, D_MODEL), w_ada=w_ada.astype(BF16), b_ada=b_ada.reshape(1, 3 * D_MODEL),
        w1=w1, qan=q_a_norm.reshape(1, Q_LORA), wq=wq, kvan=kv_a_norm.reshape(1, KV_LORA), wkv=wkv,
        gq=_tail_gain(q_norm), gk=_tail_gain(k_norm), w2=w2, w3=w3,
        conv_w=jnp.pad(conv_w, ((0, 8 - CONV_WIDTH), (0, 0))), conv_b=conv_b.reshape(1, SSM_CONV_DIM),
        par=par, dtb=dtb, dexp=dexp, nw=ssm_norm.reshape(1, SSM_INNER),
        wpa=w_proj_a.astype(BF16), wpb=w_proj_b.astype(BF16), wo=w_out.astype(BF16))


def _tiles(s):
    return dict(tm=min(512, s), tq=min(1024, s), tk=min(512, s), tc=min(256, s), cps=min(16, s // CHUNK))


def _encoder_layer(x, mod, w, cos, sin):
    b, s, _ = x.shape
    t = _tiles(s)
    shift = mod[:, :D_MODEL].reshape(b, 1, D_MODEL)
    scale = mod[:, D_MODEL:2 * D_MODEL].reshape(b, 1, D_MODEL)
    gate = mod[:, 2 * D_MODEL:].reshape(b, 1, D_MODEL)
    q, k, v = _attn_prep(x, w["ng"], scale, shift, w["w1"], w["qan"], w["wq"], w["kvan"], w["wkv"],
                         w["gq"], w["gk"], cos, sin, t["tm"])
    attn = _attention(q, k, v, t["tq"], t["tk"])
    sz, xbc, dt = _ssd_prep(x, w["ng"], scale, shift, w["w2"], w["dtb"], t["tm"])
    xact = _conv(xbc, w["conv_w"], w["conv_b"], t["tc"])
    yn = _ssd(xact, dt, sz, w["par"], w["dexp"], w["nw"], t["cps"])
    return _merge(x, w["ng"], scale, shift, gate, attn, yn, w["w3"], w["wpa"], w["wpb"], w["wo"], t["tm"])


def kernel(x_prompt, x_sample, c_prompt, c_sample, norm_g, w_ada, b_ada, w_in, q_a_norm, w_q_up, kv_a_norm, w_kv_up, q_norm, k_norm, w_proj_a, conv_w, conv_b, dt_bias_f, dt_bias_b, a_log_f, a_log_b, d_f, d_b, ssm_norm, w_proj_b, w_out):
    layer_params = (norm_g, w_ada, b_ada, w_in, q_a_norm, w_q_up, kv_a_norm, w_kv_up, q_norm, k_norm, w_proj_a,
                    conv_w, conv_b, dt_bias_f, dt_bias_b, a_log_f, a_log_b, d_f, d_b, ssm_norm, w_proj_b, w_out)
    nb = x_prompt.shape[0]
    c_all = jnp.concatenate([c_prompt, c_sample], axis=0)
    c_all = jnp.pad(c_all, ((0, -c_all.shape[0] % 8), (0, 0)))
    tables = {s: _rope_tables(s) for s in {x_prompt.shape[1], x_sample.shape[1]}}
    y_prompt, y_sample = x_prompt, x_sample
    for l in range(norm_g.shape[0]):
        w = _prep_weights(*(p[l] for p in layer_params))
        mod = _mod(c_all, w["w_ada"], w["b_ada"])
        y_prompt = _encoder_layer(y_prompt, mod[:nb], w, *tables[y_prompt.shape[1]])
        y_sample = _encoder_layer(y_sample, mod[nb:nb + x_sample.shape[0]], w, *tables[y_sample.shape[1]])
    return (y_prompt, y_sample)
```

```python
import functools
import math

import jax
import jax.numpy as jnp
from jax import lax
from jax.experimental import pallas as pl
from jax.experimental.pallas import tpu as pltpu

D_MODEL = 1024
MLA_HEADS = 8
QK_NOPE = 128
QK_ROPE = 64
QK_HEAD = QK_NOPE + QK_ROPE
V_HEAD = 128
Q_LORA = 384
KV_LORA = 256
MLA_WIDTH = MLA_HEADS * V_HEAD
ROPE_THETA = 10000.0
HEAD_PAD = 256

SSM_INNER = 2048
SSM_HEADDIM = 64
SSM_HEADS = 32
SSM_GROUPS = 4
SSM_STATE = 128
SSM_GROUP_WIDTH = SSM_INNER // SSM_GROUPS
SSM_CONV_DIM = SSM_INNER + 2 * SSM_GROUPS * SSM_STATE
CONV_WIDTH = 5
CHUNK = 128
EPS = 1e-6

LANES = 128
CONV_HALO = 16
VMEM_LIMIT = 56 * 1024 * 1024

F32 = jnp.float32
BF16 = jnp.bfloat16
NEG_BIG = -1e30
LOG2_E = 1.4426950408889634


def _dot(a, b):
    return jnp.dot(a, b, preferred_element_type=F32)


def _dot_nt(a, b):
    return lax.dot_general(a, b, (((1,), (1,)), ((), ())), preferred_element_type=F32)


def _sigmoid(t):
    return 1.0 / (1.0 + jnp.exp2(t * (-LOG2_E)))


def _silu(t):
    return t * _sigmoid(t)


def _softplus(t):
    return jnp.maximum(t, 0.0) + jnp.log1p(jnp.exp(-jnp.abs(t)))


def _rms(t):
    return t * lax.rsqrt(jnp.mean(t * t, axis=-1, keepdims=True) + EPS)


def _modulated_norm(x, g, scale, shift):
    return (_rms(x) * g) * (1.0 + scale) + shift


def _split_dot(m, v):
    hi = v.astype(BF16)
    r1 = v - hi.astype(F32)
    mid = r1.astype(BF16)
    lo = (r1 - mid.astype(F32)).astype(BF16)
    return _dot(m, hi) + _dot(m, mid) + _dot(m, lo)


def _params(*sem):
    return pltpu.CompilerParams(dimension_semantics=sem, vmem_limit_bytes=VMEM_LIMIT)


def _mod_kernel(c_ref, w_ref, b_ref, o_ref):
    o_ref[...] = _dot(_silu(c_ref[...]).astype(BF16), w_ref[...]) + b_ref[...]


def _mod(c, w_ada, b_ada):
    n = c.shape[0]
    return pl.pallas_call(
        _mod_kernel,
        grid=(3,),
        in_specs=[pl.BlockSpec((n, D_MODEL), lambda j: (0, 0)),
                  pl.BlockSpec((D_MODEL, D_MODEL), lambda j: (0, j)),
                  pl.BlockSpec((1, D_MODEL), lambda j: (0, j))],
        out_specs=pl.BlockSpec((n, D_MODEL), lambda j: (0, j)),
        out_shape=jax.ShapeDtypeStruct((n, 3 * D_MODEL), F32),
        compiler_params=_params("arbitrary"),
        name="mod",
    )(c, w_ada, b_ada)


def _attn_prep_kernel(x_ref, ng_ref, sc_ref, sh_ref, w1_ref, qan_ref, wq_ref, kvan_ref, wkv_ref,
                      gq_ref, gk_ref, cos_ref, sin_ref, q_ref, k_ref, v_ref):
    for r0 in range(0, x_ref.shape[1], ATTN_PREP_ROWS):
        _attn_prep_rows(slice(r0, r0 + ATTN_PREP_ROWS), x_ref, ng_ref, sc_ref, sh_ref, w1_ref, qan_ref, wq_ref,
                        kvan_ref, wkv_ref, gq_ref, gk_ref, cos_ref, sin_ref, q_ref, k_ref, v_ref)


ATTN_PREP_ROWS = 128


def _attn_prep_rows(rows, x_ref, ng_ref, sc_ref, sh_ref, w1_ref, qan_ref, wq_ref, kvan_ref, wkv_ref,
                    gq_ref, gk_ref, tcs_ref, tsc_ref, q_ref, k_ref, v_ref):
    hb = _modulated_norm(x_ref[0, rows, :], ng_ref[...], sc_ref[0], sh_ref[0]).astype(BF16)
    p1 = _dot(hb, w1_ref[...])
    qn = (_rms(p1[:, :Q_LORA]) * qan_ref[...]).astype(BF16)
    kvn = (_rms(p1[:, Q_LORA:Q_LORA + KV_LORA]) * kvan_ref[...]).astype(BF16)
    kr = p1[:, Q_LORA + KV_LORA:Q_LORA + KV_LORA + LANES]
    kr_sw = p1[:, Q_LORA + KV_LORA + LANES:]
    tcs = tcs_ref[rows, :]
    tsc = tsc_ref[rows, :]
    gq_nope, gk_nope = gq_ref[0:1, :], gk_ref[0:1, :]
    q_tab = tcs * gq_ref[1:2, :]
    kr_rot = kr * (tcs * gk_ref[1:2, :]) + kr_sw * (tsc * gk_ref[2:3, :])
    kr_sq = (kr * kr) * 0.5
    ones = jnp.ones((hb.shape[0], V_HEAD), BF16)
    scale = QK_HEAD ** -0.5 * LOG2_E
    for h in range(MLA_HEADS):
        c0 = h * HEAD_PAD
        qa = _dot(qn, wq_ref[:, c0:c0 + HEAD_PAD])
        q_nope = qa[:, :QK_NOPE]
        q_tail = qa[:, QK_NOPE:]
        ss = jnp.sum(q_nope * q_nope + (q_tail * q_tail) * 0.5, axis=-1, keepdims=True)
        r = lax.rsqrt(ss * (1.0 / QK_HEAD) + EPS) * scale
        q_ref[0, rows, c0:c0 + QK_NOPE] = (q_nope * gq_nope * r).astype(BF16)
        q_ref[0, rows, c0 + QK_NOPE:c0 + HEAD_PAD] = (q_tail * q_tab * r).astype(BF16)
        kv = _dot(kvn, wkv_ref[:, c0:c0 + HEAD_PAD])
        k_nope = kv[:, :QK_NOPE]
        ssk = jnp.sum(k_nope * k_nope + kr_sq, axis=-1, keepdims=True)
        rk = lax.rsqrt(ssk * (1.0 / QK_HEAD) + EPS)
        k_ref[0, rows, c0:c0 + QK_NOPE] = (k_nope * gk_nope * rk).astype(BF16)
        k_ref[0, rows, c0 + QK_NOPE:c0 + HEAD_PAD] = (kr_rot * rk).astype(BF16)
        v_ref[0, rows, 2 * h * V_HEAD:(2 * h + 1) * V_HEAD] = kv[:, QK_NOPE:].astype(BF16)
        v_ref[0, rows, (2 * h + 1) * V_HEAD:(2 * h + 2) * V_HEAD] = ones


def _attn_prep(x, ng, scale, shift, w1, qan, wq, kvan, wkv, gq, gk, cos, sin, tm):
    b, s, _ = x.shape
    full = lambda a: pl.BlockSpec(a.shape, lambda i, j: (0,) * a.ndim)
    per_batch = pl.BlockSpec((1, 1, D_MODEL), lambda i, j: (i, 0, 0))
    return pl.pallas_call(
        _attn_prep_kernel,
        grid=(b, s // tm),
        in_specs=[pl.BlockSpec((1, tm, D_MODEL), lambda i, j: (i, j, 0)),
                  full(ng), per_batch, per_batch, full(w1), full(qan), full(wq), full(kvan), full(wkv),
                  full(gq), full(gk),
                  pl.BlockSpec((tm, LANES), lambda i, j: (j, 0)),
                  pl.BlockSpec((tm, LANES), lambda i, j: (j, 0))],
        out_specs=[pl.BlockSpec((1, tm, MLA_HEADS * HEAD_PAD), lambda i, j: (i, j, 0)),
                   pl.BlockSpec((1, tm, MLA_HEADS * HEAD_PAD), lambda i, j: (i, j, 0)),
                   pl.BlockSpec((1, tm, 2 * MLA_WIDTH), lambda i, j: (i, j, 0))],
        out_shape=[jax.ShapeDtypeStruct((b, s, MLA_HEADS * HEAD_PAD), BF16),
                   jax.ShapeDtypeStruct((b, s, MLA_HEADS * HEAD_PAD), BF16),
                   jax.ShapeDtypeStruct((b, s, 2 * MLA_WIDTH), BF16)],
        compiler_params=_params("arbitrary", "arbitrary"),
        name="attn_prep",
    )(x, ng, scale, shift, w1, qan, wq, kvan, wkv, gq, gk, cos, sin)


ATTN_UNROLL = 16


def _attention_kernel(q_ref, k_ref, v_ref, o_ref, m_ref, acc_ref, *, tk, unroll):
    m_ref[...] = jnp.full(m_ref.shape, NEG_BIG, F32)
    acc_ref[...] = jnp.zeros(acc_ref.shape, F32)
    lane_tiles = tk // LANES

    def body(j, carry):
        r0 = pl.multiple_of(j * tk, tk)
        s = _dot_nt(q_ref[0], k_ref[0, pl.ds(r0, tk), :])
        tiles = [s[:, c * LANES:(c + 1) * LANES] for c in range(lane_tiles)]
        tile_max = functools.reduce(jnp.maximum, tiles)
        m_old = m_ref[...]
        m_new = jnp.maximum(m_old, jnp.max(tile_max, axis=1, keepdims=True))
        alpha = jnp.exp2(m_old - m_new)
        p = jnp.concatenate([jnp.exp2(t - m_new).astype(BF16) for t in tiles], axis=1)
        pv = _dot(p, v_ref[0, pl.ds(r0, tk), :])
        acc_ref[:, :V_HEAD] = alpha * acc_ref[:, :V_HEAD] + pv[:, :V_HEAD]
        acc_ref[:, V_HEAD:] = alpha * acc_ref[:, V_HEAD:] + pv[:, V_HEAD:]
        m_ref[...] = m_new
        return carry

    lax.fori_loop(0, k_ref.shape[1] // tk, body, 0, unroll=unroll)
    o_ref[0] = (acc_ref[:, :V_HEAD] / acc_ref[:, V_HEAD:]).astype(BF16)


def _attention(q, k, v, tq, tk):
    b, s, _ = q.shape
    return pl.pallas_call(
        functools.partial(_attention_kernel, tk=tk, unroll=min(ATTN_UNROLL, s // tk)),
        grid=(b, MLA_HEADS, s // tq),
        in_specs=[pl.BlockSpec((1, tq, HEAD_PAD), lambda i, h, j: (i, j, h)),
                  pl.BlockSpec((1, s, HEAD_PAD), lambda i, h, j: (i, 0, h)),
                  pl.BlockSpec((1, s, 2 * V_HEAD), lambda i, h, j: (i, 0, h))],
        out_specs=pl.BlockSpec((1, tq, V_HEAD), lambda i, h, j: (i, j, h)),
        out_shape=jax.ShapeDtypeStruct((b, s, MLA_WIDTH), BF16),
        scratch_shapes=[pltpu.VMEM((tq, LANES), F32), pltpu.VMEM((tq, 2 * V_HEAD), F32)],
        compiler_params=_params("arbitrary", "arbitrary", "arbitrary"),
        name="attention",
    )(q, k, v)


SSD_PREP_COLS = 512


def _ssd_prep_kernel(x_ref, ng_ref, sc_ref, sh_ref, w2_ref, dtb_ref, sz_ref, xbc_ref, dt_ref):
    hb = _modulated_norm(x_ref[0], ng_ref[...], sc_ref[0], sh_ref[0]).astype(BF16)
    for j in range(SSM_INNER // SSD_PREP_COLS):
        c0 = j * SSD_PREP_COLS
        sz_ref[0, :, c0:c0 + SSD_PREP_COLS] = _silu(_dot(hb, w2_ref[:, c0:c0 + SSD_PREP_COLS])).astype(BF16)
    for j in range(SSM_CONV_DIM // SSD_PREP_COLS):
        c0 = j * SSD_PREP_COLS
        w0 = SSM_INNER + c0
        xbc_ref[0, :, c0:c0 + SSD_PREP_COLS] = _dot(hb, w2_ref[:, w0:w0 + SSD_PREP_COLS]).astype(BF16)
    w0 = SSM_INNER + SSM_CONV_DIM
    dt_ref[0] = _softplus(_dot(hb, w2_ref[:, w0:w0 + LANES]) + dtb_ref[...])


def _ssd_prep(x, ng, scale, shift, w2, dtb, tm):
    b, s, _ = x.shape
    full = lambda a: pl.BlockSpec(a.shape, lambda i, j: (0,) * a.ndim)
    per_batch = pl.BlockSpec((1, 1, D_MODEL), lambda i, j: (i, 0, 0))
    row = lambda w: pl.BlockSpec((1, tm, w), lambda i, j: (i, j, 0))
    return pl.pallas_call(
        _ssd_prep_kernel,
        grid=(b, s // tm),
        in_specs=[row(D_MODEL), full(ng), per_batch, per_batch, full(w2), full(dtb)],
        out_specs=[row(SSM_INNER), row(SSM_CONV_DIM), row(LANES)],
        out_shape=[jax.ShapeDtypeStruct((b, s, SSM_INNER), BF16),
                   jax.ShapeDtypeStruct((b, s, SSM_CONV_DIM), BF16),
                   jax.ShapeDtypeStruct((b, s, LANES), F32)],
        compiler_params=_params("arbitrary", "arbitrary"),
        name="ssd_prep",
    )(x, ng, scale, shift, w2, dtb)


CONV_COLS = 512
CONV_ROW_BLOCK = 64

CONV_OFFSETS = tuple(d for d in range(-(CONV_WIDTH // 2), CONV_WIDTH // 2 + 1) if d != 0)


def _conv_columns(c0, prev_ref, cur_ref, next_ref, sh_ref, w_ref, b_ref, o_ref):
    j = pl.program_id(1)
    tc = cur_ref.shape[1]
    h = CONV_HALO
    mid = CONV_WIDTH // 2
    cols = slice(c0, c0 + CONV_COLS)
    taps = [w_ref[w:w + 1, cols] for w in range(CONV_WIDTH)]
    bias = b_ref[:, cols]
    cur = cur_ref[0, :, cols]
    prev = jnp.where(j > 0, prev_ref[0, :, cols].astype(F32), 0.0)
    nxt = jnp.where(j < pl.num_programs(1) - 1, next_ref[0, :, cols].astype(F32), 0.0)
    top_src = jnp.concatenate([prev, cur[0:h].astype(F32)], axis=0)
    bot_src = jnp.concatenate([cur[tc - h:tc].astype(F32), nxt], axis=0)
    t = lax.broadcasted_iota(jnp.int32, (h, 1), 0)
    top_fix = jnp.zeros((h, CONV_COLS), F32)
    bot_fix = jnp.zeros((h, CONV_COLS), F32)
    for d in CONV_OFFSETS:
        if d < 0:
            top_fix = top_fix + jnp.where(t + d < 0, pltpu.roll(top_src, -d, 0)[h:2 * h], 0.0) * taps[mid + d]
        else:
            bot_fix = bot_fix + jnp.where(t + d >= h, pltpu.roll(bot_src, 2 * h - d, 0)[0:h], 0.0) * taps[mid + d]
    nd = len(CONV_OFFSETS)
    rb = CONV_ROW_BLOCK
    for blk in range(tc // rb):
        r0 = blk * rb
        shifted = _dot(sh_ref[nd * r0:nd * (r0 + rb), :], cur)
        acc = cur[r0:r0 + rb].astype(F32) * taps[mid] + bias
        for i, d in enumerate(CONV_OFFSETS):
            acc = acc + shifted[i * rb:(i + 1) * rb] * taps[mid + d]
        lo, hi = 0, rb
        if blk == 0:
            o_ref[0, 0:h, cols] = _silu(acc[0:h] + top_fix).astype(BF16)
            lo = h
        if blk == tc // rb - 1:
            o_ref[0, tc - h:tc, cols] = _silu(acc[rb - h:rb] + bot_fix).astype(BF16)
            hi = rb - h
        o_ref[0, r0 + lo:r0 + hi, cols] = _silu(acc[lo:hi]).astype(BF16)


def _conv_kernel(*refs):
    for c0 in range(0, SSM_CONV_DIM, CONV_COLS):
        _conv_columns(c0, *refs)


def _conv(xbc, conv_w, conv_b, tc):
    b, s, _ = xbc.shape
    hb = tc // CONV_HALO
    last = s // CONV_HALO - 1
    src = jnp.arange(tc)[None, :]
    shifts = jnp.concatenate(
        [(src == jnp.arange(r0, r0 + CONV_ROW_BLOCK)[:, None] + d)
         for r0 in range(0, tc, CONV_ROW_BLOCK) for d in CONV_OFFSETS], axis=0).astype(BF16)
    full = lambda a: pl.BlockSpec(a.shape, lambda i, j: (0,) * a.ndim)
    return pl.pallas_call(
        _conv_kernel,
        grid=(b, s // tc),
        in_specs=[pl.BlockSpec((1, CONV_HALO, SSM_CONV_DIM), lambda i, j: (i, jnp.maximum(j * hb - 1, 0), 0)),
                  pl.BlockSpec((1, tc, SSM_CONV_DIM), lambda i, j: (i, j, 0)),
                  pl.BlockSpec((1, CONV_HALO, SSM_CONV_DIM), lambda i, j: (i, jnp.minimum((j + 1) * hb, last), 0)),
                  full(shifts), full(conv_w), full(conv_b)],
        out_specs=pl.BlockSpec((1, tc, SSM_CONV_DIM), lambda i, j: (i, j, 0)),
        out_shape=jax.ShapeDtypeStruct((b, s, SSM_CONV_DIM), BF16),
        compiler_params=_params("arbitrary", "arbitrary"),
        name="conv",
    )(xbc, xbc, xbc, shifts, conv_w, conv_b)


HEADS_PER_GROUP = SSM_HEADS // SSM_GROUPS
SSD_UNROLL = 8


def _ssd_sweep(fwd, x_ref, b_ref, c_ref, dt_ref, sz_ref, par_ref, d_ref, nw_ref, o_ref, st_ref, yacc_ref,
               cps, nsteps):
    g = pl.program_id(1)
    s = pl.program_id(3)
    rows = cps * CHUNK
    sidx = s if fwd else nsteps - 1 - s
    shift = lax.rem(LANES - ((0 if fwd else SSM_HEADS) + g * HEADS_PER_GROUP), LANES)
    par = pltpu.roll(par_ref[...], shift, 1)
    a_coef = -jnp.exp(par[0:1, :]) * LOG2_E
    row = lax.broadcasted_iota(jnp.int32, (CHUNK, CHUNK), 0)
    col = lax.broadcasted_iota(jnp.int32, (CHUNK, CHUNK), 1)
    tri_mask = (row >= col) if fwd else (row <= col)
    tri = jnp.where(tri_mask, 1.0, 0.0).astype(BF16)
    low_half = col < SSM_HEADDIM
    lane = lax.broadcasted_iota(jnp.int32, (1, LANES), 1)
    lane_lo = jnp.where(lane < SSM_HEADDIM, 1.0, 0.0).astype(BF16)
    lane_hi = jnp.where(lane < SSM_HEADDIM, 0.0, 1.0).astype(BF16)
    erow = lax.broadcasted_iota(jnp.int32, (LANES, SSM_GROUP_WIDTH), 0)
    ecol = lax.broadcasted_iota(jnp.int32, (LANES, SSM_GROUP_WIDTH), 1)
    expand = jnp.where((ecol >= erow * SSM_HEADDIM) & (ecol < (erow + 1) * SSM_HEADDIM), 1.0, 0.0).astype(BF16)
    d_row = d_ref[0:1, :] if fwd else d_ref[1:2, :]
    far = CHUNK - 1 if fwd else 0

    def chunk(r0):
        xb = x_ref[0, pl.ds(r0, CHUNK), :]
        bm = b_ref[0, pl.ds(r0, CHUNK), :]
        cm = c_ref[0, pl.ds(r0, CHUNK), :]
        dt = pltpu.roll(dt_ref[0, pl.ds(r0, CHUNK), :], shift, 1)
        a = dt * a_coef
        acum = _split_dot(tri, a)
        acum_t = acum.T[0:HEADS_PER_GROUP, :]
        dt_t = dt.T[0:HEADS_PER_GROUP, :]
        alast = acum[far:far + 1, :]
        w_t = dt_t * jnp.exp2(acum_t[:, far:far + 1] - acum_t)
        src_t = acum_t - jnp.log2(dt_t)
        cdec = jnp.exp2(_split_dot(jnp.broadcast_to(alast, (8, LANES)), expand)[0:1, :])
        cb = _dot_nt(cm, bm)
        bm_t = bm.astype(F32).T
        st = st_ref[...]
        yoff = _dot(cm, st.astype(BF16))
        y_parts = []
        for p in range(HEADS_PER_GROUP // 2):
            c0 = p * LANES
            xp = xb[:, c0:c0 + LANES]

            def head_y(h):
                reach = jnp.broadcast_to(acum[:, h:h + 1], (CHUNK, CHUNK))
                mix = cb * jnp.exp2(jnp.where(tri_mask, reach - src_t[h:h + 1, :], NEG_BIG))
                return _dot(mix.astype(BF16), xp), reach

            y0, reach0 = head_y(2 * p)
            y1, reach1 = head_y(2 * p + 1)
            y_pair = (jnp.where(low_half, y0, y1)
                      + yoff[:, c0:c0 + LANES] * jnp.exp2(jnp.where(low_half, reach0, reach1)))
            y_parts.append(y_pair + xp.astype(F32) * d_row[:, c0:c0 + LANES])
            s_lhs = jnp.concatenate([(bm_t * w_t[2 * p:2 * p + 1, :]).astype(BF16),
                                     (bm_t * w_t[2 * p + 1:2 * p + 2, :]).astype(BF16)], axis=1)
            s_rhs = jnp.concatenate([xp * lane_lo, xp * lane_hi], axis=0)
            st_ref[:, c0:c0 + LANES] = st[:, c0:c0 + LANES] * cdec[:, c0:c0 + LANES] + _dot(s_lhs, s_rhs)
        y = jnp.concatenate(y_parts, axis=1)
        grow = pl.multiple_of(sidx * rows + r0, CHUNK)
        if fwd:
            yacc_ref[pl.ds(grow, CHUNK), :] = y
        else:
            yt = (y + yacc_ref[pl.ds(grow, CHUNK), :]) * sz_ref[0, pl.ds(r0, CHUNK), :].astype(F32)
            o_ref[0, pl.ds(r0, CHUNK), :] = (_rms(yt) * nw_ref[...]).astype(BF16)

    unroll = min(SSD_UNROLL, cps)
    groups = cps // unroll

    def group(gi, carry):
        base = (gi if fwd else groups - 1 - gi) * (unroll * CHUNK)
        for i in range(unroll):
            chunk(pl.multiple_of(base + (i if fwd else unroll - 1 - i) * CHUNK, CHUNK))
        return carry

    lax.fori_loop(0, groups, group, 0)


def _ssd_kernel(*refs, cps, nsteps):
    st_ref = refs[-2]
    ph = pl.program_id(2)

    @pl.when(pl.program_id(3) == 0)
    def _():
        st_ref[...] = jnp.zeros(st_ref.shape, F32)

    @pl.when(ph == 0)
    def _():
        _ssd_sweep(True, *refs, cps, nsteps)

    @pl.when(ph == 1)
    def _():
        _ssd_sweep(False, *refs, cps, nsteps)


def _ssd(xact, dt, sz, par, dexp, nw, cps):
    b, s, _ = xact.shape
    rows = cps * CHUNK
    nsteps = s // rows
    cur = lambda ph, t: jnp.where(ph == 0, t, nsteps - 1 - t)
    late = lambda ph, t: jnp.where(ph == 0, nsteps - 1, nsteps - 1 - t)
    gw = SSM_GROUP_WIDTH // LANES
    b_blk = SSM_INNER // SSM_STATE
    c_blk = b_blk + SSM_GROUPS
    return pl.pallas_call(
        functools.partial(_ssd_kernel, cps=cps, nsteps=nsteps),
        grid=(b, SSM_GROUPS, 2, nsteps),
        in_specs=[pl.BlockSpec((1, rows, SSM_GROUP_WIDTH), lambda i, g, ph, t: (i, cur(ph, t), g)),
                  pl.BlockSpec((1, rows, SSM_STATE), lambda i, g, ph, t: (i, cur(ph, t), b_blk + g)),
                  pl.BlockSpec((1, rows, SSM_STATE), lambda i, g, ph, t: (i, cur(ph, t), c_blk + g)),
                  pl.BlockSpec((1, rows, LANES), lambda i, g, ph, t: (i, cur(ph, t), 0)),
                  pl.BlockSpec((1, rows, SSM_GROUP_WIDTH), lambda i, g, ph, t: (i, late(ph, t), g)),
                  pl.BlockSpec((8, LANES), lambda i, g, ph, t: (0, 0)),
                  pl.BlockSpec((8, SSM_GROUP_WIDTH), lambda i, g, ph, t: (0, g)),
                  pl.BlockSpec((1, SSM_GROUP_WIDTH), lambda i, g, ph, t: (0, g))],
        out_specs=pl.BlockSpec((1, rows, SSM_GROUP_WIDTH), lambda i, g, ph, t: (i, late(ph, t), g)),
        out_shape=jax.ShapeDtypeStruct((b, s, SSM_INNER), BF16),
        scratch_shapes=[pltpu.VMEM((SSM_STATE, SSM_GROUP_WIDTH), F32),
                        pltpu.VMEM((s, SSM_GROUP_WIDTH), F32)],
        compiler_params=_params("arbitrary", "arbitrary", "arbitrary", "arbitrary"),
        name="ssd",
    )(xact, xact, xact, dt, sz, par, dexp, nw)


def _merge_kernel(x_ref, ng_ref, sc_ref, sh_ref, gt_ref, attn_ref, yn_ref, w3_ref, wpa_ref, wpb_ref, wo_ref,
                  o_ref):
    x = x_ref[0]
    hb = _modulated_norm(x, ng_ref[...], sc_ref[0], sh_ref[0]).astype(BF16)
    gate_a = _dot(hb, w3_ref[:, :MLA_WIDTH])
    a_in = (attn_ref[0].astype(F32) * _silu(gate_a)).astype(BF16)
    branch_a = _dot(a_in, wpa_ref[...])
    branch_b = _dot(yn_ref[0], wpb_ref[...])
    g_a = _dot(hb, w3_ref[:, MLA_WIDTH:MLA_WIDTH + D_MODEL])
    g_b = _dot(hb, w3_ref[:, MLA_WIDTH + D_MODEL:])
    merged = _sigmoid(g_a) * branch_a + _sigmoid(g_b) * branch_b
    out = _dot(merged.astype(BF16), wo_ref[...])
    o_ref[0] = x + gt_ref[0] * out


def _merge(x, ng, scale, shift, gate, attn, yn, w3, wpa, wpb, wo, tm):
    b, s, _ = x.shape
    full = lambda a: pl.BlockSpec(a.shape, lambda i, j: (0,) * a.ndim)
    per_batch = pl.BlockSpec((1, 1, D_MODEL), lambda i, j: (i, 0, 0))
    row = lambda w: pl.BlockSpec((1, tm, w), lambda i, j: (i, j, 0))
    return pl.pallas_call(
        _merge_kernel,
        grid=(b, s // tm),
        in_specs=[row(D_MODEL), full(ng), per_batch, per_batch, per_batch, row(MLA_WIDTH), row(SSM_INNER),
                  full(w3), full(wpa), full(wpb), full(wo)],
        out_specs=row(D_MODEL),
        out_shape=jax.ShapeDtypeStruct((b, s, D_MODEL), F32),
        compiler_params=_params("arbitrary", "arbitrary"),
        name="merge",
    )(x, ng, scale, shift, gate, attn, yn, w3, wpa, wpb, wo)


def _rope_tables(s):
    half = QK_ROPE // 2
    inv_freq = jnp.exp(-math.log(ROPE_THETA) * jnp.arange(half, dtype=F32) / half)
    ang = jnp.arange(s, dtype=F32)[:, None] * inv_freq[None, :]
    cos, sin = jnp.cos(ang), jnp.sin(ang)
    c = jnp.concatenate([cos, cos], axis=1)
    sg = jnp.concatenate([-sin, sin], axis=1)
    return jnp.concatenate([c, sg], axis=1), jnp.concatenate([sg, c], axis=1)


def _swap_halves(t):
    half = QK_ROPE // 2
    return jnp.concatenate([t[..., half:], t[..., :half]], axis=-1)


def _tail_gain(g):
    gr = g[QK_NOPE:]
    return jnp.stack([g[:QK_NOPE], jnp.concatenate([gr, _swap_halves(gr)]), jnp.concatenate([_swap_halves(gr), gr])])


def _prep_weights(norm_g, w_ada, b_ada, w_in, q_a_norm, w_q_up, kv_a_norm, w_kv_up, q_norm, k_norm, w_proj_a,
                  conv_w, conv_b, dt_bias_f, dt_bias_b, a_log_f, a_log_b, d_f, d_b, ssm_norm, w_proj_b, w_out):
    o_ga = Q_LORA + KV_LORA + QK_ROPE
    o_z = o_ga + MLA_WIDTH
    o_xbc = o_z + SSM_INNER
    o_dt = o_xbc + SSM_CONV_DIM
    o_gm = o_dt + 2 * SSM_HEADS
    zpad = jnp.zeros((D_MODEL, LANES - QK_ROPE), F32)
    w_kr = w_in[:, o_ga - QK_ROPE:o_ga]
    w1 = jnp.concatenate([w_in[:, :o_ga], _swap_halves(w_kr), _swap_halves(w_kr), w_kr], axis=1).astype(BF16)
    wq = w_q_up.reshape(Q_LORA, MLA_HEADS, QK_HEAD)
    wq = jnp.concatenate([wq, _swap_halves(wq[..., QK_NOPE:])], axis=-1)
    wq = wq.reshape(Q_LORA, MLA_HEADS * HEAD_PAD).astype(BF16)
    wkv = w_kv_up.astype(BF16)
    w2 = jnp.concatenate([w_in[:, o_z:o_gm], zpad], axis=1).astype(BF16)
    w3 = jnp.concatenate([w_in[:, o_ga:o_z], w_in[:, o_gm:]], axis=1).astype(BF16)
    lane_pad = jnp.zeros((LANES - 2 * SSM_HEADS,), F32)
    par = jnp.zeros((8, LANES), F32)
    par = par.at[0].set(jnp.concatenate([a_log_f, a_log_b, lane_pad]))
    dtb = jnp.concatenate([dt_bias_f, dt_bias_b, lane_pad]).reshape(1, LANES)
    dexp = jnp.zeros((8, SSM_INNER), F32)
    dexp = dexp.at[0].set(jnp.repeat(d_f, SSM_HEADDIM)).at[1].set(jnp.repeat(d_b, SSM_HEADDIM))
    return dict(
        ng=norm_g.reshape(1, D_MODEL), w_ada=w_ada.astype(BF16), b_ada=b_ada.reshape(1, 3 * D_MODEL),
        w1=w1, qan=q_a_norm.reshape(1, Q_LORA), wq=wq, kvan=kv_a_norm.reshape(1, KV_LORA), wkv=wkv,
        gq=_tail_gain(q_norm), gk=_tail_gain(k_norm), w2=w2, w3=w3,
        conv_w=jnp.pad(conv_w, ((0, 8 - CONV_WIDTH), (0, 0))), conv_b=conv_b.reshape(1, SSM_CONV_DIM),
        par=par, dtb=dtb, dexp=dexp, nw=ssm_norm.reshape(1, SSM_INNER),
        wpa=w_proj_a.astype(BF16), wpb=w_proj_b.astype(BF16), wo=w_out.astype(BF16))


def _tiles(s):
    return dict(tm=min(512, s), tq=min(1024, s), tk=min(512, s), tc=min(256, s), cps=min(16, s // CHUNK))


def _encoder_layer(x, mod, w, cos, sin):
    b, s, _ = x.shape
    t = _tiles(s)
    shift = mod[:, :D_MODEL].reshape(b, 1, D_MODEL)
    scale = mod[:, D_MODEL:2 * D_MODEL].reshape(b, 1, D_MODEL)
    gate = mod[:, 2 * D_MODEL:].reshape(b, 1, D_MODEL)
    q, k, v = _attn_prep(x, w["ng"], scale, shift, w["w1"], w["qan"], w["wq"], w["kvan"], w["wkv"],
                         w["gq"], w["gk"], cos, sin, t["tm"])
    attn = _attention(q, k, v, t["tq"], t["tk"])
    sz, xbc, dt = _ssd_prep(x, w["ng"], scale, shift, w["w2"], w["dtb"], t["tm"])
    xact = _conv(xbc, w["conv_w"], w["conv_b"], t["tc"])
    yn = _ssd(xact, dt, sz, w["par"], w["dexp"], w["nw"], t["cps"])
    return _merge(x, w["ng"], scale, shift, gate, attn, yn, w["w3"], w["wpa"], w["wpb"], w["wo"], t["tm"])


def kernel(x_prompt, x_sample, c_prompt, c_sample, norm_g, w_ada, b_ada, w_in, q_a_norm, w_q_up, kv_a_norm, w_kv_up, q_norm, k_norm, w_proj_a, conv_w, conv_b, dt_bias_f, dt_bias_b, a_log_f, a_log_b, d_f, d_b, ssm_norm, w_proj_b, w_out):
    layer_params = (norm_g, w_ada, b_ada, w_in, q_a_norm, w_q_up, kv_a_norm, w_kv_up, q_norm, k_norm, w_proj_a,
                    conv_w, conv_b, dt_bias_f, dt_bias_b, a_log_f, a_log_b, d_f, d_b, ssm_norm, w_proj_b, w_out)
    nb = x_prompt.shape[0]
    c_all = jnp.concatenate([c_prompt, c_sample], axis=0)
    c_all = jnp.pad(c_all, ((0, -c_all.shape[0] % 8), (0, 0)))
    tables = {s: _rope_tables(s) for s in {x_prompt.shape[1], x_sample.shape[1]}}
    y_prompt, y_sample = x_prompt, x_sample
    for l in range(norm_g.shape[0]):
        w = _prep_weights(*(p[l] for p in layer_params))
        mod = _mod(c_all, w["w_ada"], w["b_ada"])
        y_prompt = _encoder_layer(y_prompt, mod[:nb], w, *tables[y_prompt.shape[1]])
        y_sample = _encoder_layer(y_sample, mod[nb:nb + x_sample.shape[0]], w, *tables[y_sample.shape[1]])
    return (y_prompt, y_sample)
```
